```python
import math
import jax
import jax.numpy as jnp
from jax import lax
import numpy as np

D_MODEL = 1024
BATCH = 4
SEQ = 8192
DEPTH = 1

CHUNK = 64
Q_BLOCK = 128
ATTN_WIDTH = 1024
SSM_WIDTH = 1024
MIX_WIDTH = ATTN_WIDTH + SSM_WIDTH
HEAD_DIM = 64
V_HEAD_DIM = 2 * HEAD_DIM
ATTN_HEADS = ATTN_WIDTH // V_HEAD_DIM
SSM_GROUP = 16
SSM_GROUPS = SSM_WIDTH // SSM_GROUP
SSM_STATE = 64
ROPE_THETA = 10000.0
NORM_EPS = 1e-6
MASK_VALUE = -1e30
IN_SPLITS = (ATTN_WIDTH, 2 * ATTN_WIDTH, 3 * ATTN_WIDTH, 4 * ATTN_WIDTH, 4 * ATTN_WIDTH + SSM_WIDTH)
IN_COLS = 4 * ATTN_WIDTH + 2 * SSM_WIDTH

kernel_name = "hybrid_diffattn_s5_adaln_layer"


def rms_norm(x, g):
    xf = x.astype(jnp.float32)
    y = xf * lax.rsqrt(jnp.mean(xf * xf, axis=-1, keepdims=True) + NORM_EPS)
    return (y * g.astype(jnp.float32)).astype(x.dtype)


def rope(x, cos, sin):
    half = x.shape[-1] // 2
    x1, x2 = x[..., :half], x[..., half:]
    return jnp.concatenate([x1 * cos - x2 * sin, x2 * cos + x1 * sin], axis=-1)


def diff_attention(q, k, v, lam):
    b_, h_, _, s_, d = q.shape
    nblk = s_ // Q_BLOCK
    scale = d ** -0.5
    qb = q.reshape(b_, h_, 2, nblk, Q_BLOCK, d).transpose(3, 0, 1, 2, 4, 5)
    k_chunk = jnp.arange(s_) // CHUNK

    def one_block(args):
        i, qi = args
        s = jnp.einsum('bhmqd,bhmkd->bhmqk', qi, k).astype(jnp.float32) * scale
        q_chunk = (i * Q_BLOCK + jnp.arange(Q_BLOCK)) // CHUNK
        allowed = k_chunk[None, :] <= q_chunk[:, None]
        s = jnp.where(allowed, s, MASK_VALUE)
        p = jax.nn.softmax(s, axis=-1)
        w = p[:, :, 0] - lam * p[:, :, 1]
        return jnp.einsum('bhqk,bhkv->bhqv', w.astype(v.dtype), v)

    out = lax.map(one_block, (jnp.arange(nblk), qb))
    return out.transpose(1, 0, 3, 2, 4).reshape(b_, s_, h_, v.shape[-1])


def _complex_affine_combine(e1, e2):
    a1r, a1i, b1r, b1i = e1
    a2r, a2i, b2r, b2i = e2
    ar = a2r * a1r - a2i * a1i
    ai = a2r * a1i + a2i * a1r
    br = a2r * b1r - a2i * b1i + b2r
    bi = a2r * b1i + a2i * b1r + b2i
    return (ar, ai, br, bi)


def s5_ssm(u, a_re, a_im, log_dt, b_re, b_im, c_re, c_im, d_skip):
    b_, s_, w_ = u.shape
    n_chunks = s_ // CHUNK
    f32 = jnp.float32
    uf = u.astype(f32)
    uc = uf.reshape(b_, n_chunks, CHUNK, SSM_GROUPS, SSM_GROUP).transpose(1, 0, 2, 3, 4)
    dt = jnp.exp(log_dt.astype(f32))[:, None]
    ar, ai = a_re.astype(f32), a_im.astype(f32)
    mag = jnp.exp(ar * dt)
    abar_re, abar_im = mag * jnp.cos(ai * dt), mag * jnp.sin(ai * dt)
    nr, ni = abar_re - 1.0, abar_im
    den = ar * ar + ai * ai
    f_re = (nr * ar + ni * ai) / den
    f_im = (ni * ar - nr * ai) / den
    br_, bi_ = b_re.astype(f32), b_im.astype(f32)
    bb_re = f_re[..., None] * br_ - f_im[..., None] * bi_
    bb_im = f_re[..., None] * bi_ + f_im[..., None] * br_
    cr, ci = c_re.astype(f32), c_im.astype(f32)
    a_seq_re = jnp.broadcast_to(abar_re[None, None], (1, CHUNK, SSM_GROUPS, SSM_STATE))
    a_seq_im = jnp.broadcast_to(abar_im[None, None], (1, CHUNK, SSM_GROUPS, SSM_STATE))

    def chunk_step(carry, u_c):
        prev_re, prev_im = carry
        bu_re = jnp.einsum('bcgh,gph->bcgp', u_c, bb_re)
        bu_im = jnp.einsum('bcgh,gph->bcgp', u_c, bb_im)
        acum_re, acum_im, h_re, h_im = lax.associative_scan(
            _complex_affine_combine, (a_seq_re, a_seq_im, bu_re, bu_im), axis=1)
        h_re = h_re + acum_re * prev_re[:, None] - acum_im * prev_im[:, None]
        h_im = h_im + acum_re * prev_im[:, None] + acum_im * prev_re[:, None]
        y = jnp.einsum('bcgp,ghp->bcgh', h_re, cr) - jnp.einsum('bcgp,ghp->bcgh', h_im, ci)
        return (h_re[:, -1], h_im[:, -1]), y

    init = (jnp.zeros((b_, SSM_GROUPS, SSM_STATE), f32), jnp.zeros((b_, SSM_GROUPS, SSM_STATE), f32))
    _, ys = lax.scan(chunk_step, init, uc)
    y = ys.transpose(1, 0, 2, 3, 4).reshape(b_, s_, w_)
    y = y + d_skip.astype(f32) * uf
    return y.astype(u.dtype)


def setup_inputs(seed: int = 0) -> dict:
    key = jax.random.key(seed)
    ks = jax.random.split(key, 24)
    f32 = jnp.float32

    def nrm(k, shape, scale):
        return jax.random.normal(k, shape, f32) * scale

    n_idx = jnp.arange(SSM_STATE, dtype=f32)
    gps = (DEPTH, SSM_GROUPS, SSM_STATE)
    return {
        "x": nrm(ks[0], (BATCH, SEQ, D_MODEL), 1.0),
        "c": nrm(ks[1], (BATCH, D_MODEL), 1.0),
        "w_ada": nrm(ks[2], (DEPTH, D_MODEL, 3 * D_MODEL), 0.5 * D_MODEL ** -0.5),
        "b_ada": nrm(ks[3], (DEPTH, 3 * D_MODEL), 0.02),
        "norm_g": 1.0 + nrm(ks[4], (DEPTH, D_MODEL), 0.02),
        "w_in": nrm(ks[5], (DEPTH, D_MODEL, IN_COLS), D_MODEL ** -0.5),
        "q_norm_g": 1.0 + nrm(ks[6], (DEPTH, HEAD_DIM), 0.02),
        "k_norm_g": 1.0 + nrm(ks[7], (DEPTH, HEAD_DIM), 0.02),
        "lam_q1": nrm(ks[8], (DEPTH, HEAD_DIM), 0.1),
        "lam_k1": nrm(ks[9], (DEPTH, HEAD_DIM), 0.1),
        "lam_q2": nrm(ks[10], (DEPTH, HEAD_DIM), 0.1),
        "lam_k2": nrm(ks[11], (DEPTH, HEAD_DIM), 0.1),
        "head_norm_g": 1.0 + nrm(ks[12], (DEPTH, V_HEAD_DIM), 0.02),
        "ssm_a_re": -0.5 + nrm(ks[13], gps, 0.01),
        "ssm_a_im": math.pi * n_idx + nrm(ks[14], gps, 0.01),
        "ssm_log_dt": jax.random.uniform(ks[15], (DEPTH, SSM_GROUPS), f32, math.log(1e-3), math.log(1e-1)),
        "ssm_b_re": nrm(ks[16], (DEPTH, SSM_GROUPS, SSM_STATE, SSM_GROUP), (2 * SSM_GROUP) ** -0.5),
        "ssm_b_im": nrm(ks[17], (DEPTH, SSM_GROUPS, SSM_STATE, SSM_GROUP), (2 * SSM_GROUP) ** -0.5),
        "ssm_c_re": nrm(ks[18], (DEPTH, SSM_GROUPS, SSM_GROUP, SSM_STATE), SSM_STATE ** -0.5),
        "ssm_c_im": nrm(ks[19], (DEPTH, SSM_GROUPS, SSM_GROUP, SSM_STATE), SSM_STATE ** -0.5),
        "ssm_d": nrm(ks[20], (DEPTH, SSM_WIDTH), 1.0),
        "w_glu": nrm(ks[21], (DEPTH, SSM_WIDTH, SSM_WIDTH), SSM_WIDTH ** -0.5),
        "b_glu": nrm(ks[22], (DEPTH, SSM_WIDTH), 0.02),
        "w_out": nrm(ks[23], (DEPTH, MIX_WIDTH, D_MODEL), MIX_WIDTH ** -0.5),
    }


def reference(x, c, w_ada, b_ada, norm_g, w_in, q_norm_g, k_norm_g, lam_q1, lam_k1, lam_q2, lam_k2,
              head_norm_g, ssm_a_re, ssm_a_im, ssm_log_dt, ssm_b_re, ssm_b_im, ssm_c_re, ssm_c_im,
              ssm_d, w_glu, b_glu, w_out):
    b_, s_, _ = x.shape
    f32 = jnp.float32
    pos = jnp.arange(s_, dtype=f32)
    inv_freq = 1.0 / (ROPE_THETA ** (jnp.arange(0, HEAD_DIM, 2, dtype=f32) / HEAD_DIM))
    ang = pos[:, None] * inv_freq[None, :]
    cos, sin = jnp.cos(ang).astype(x.dtype), jnp.sin(ang).astype(x.dtype)

    for l in range(DEPTH):
        lam_init = 0.8 - 0.6 * math.exp(-0.3 * l)
        mod = jax.nn.silu(c) @ w_ada[l] + b_ada[l]
        shift, scale, gate = jnp.split(mod, 3, axis=-1)
        h = rms_norm(x, norm_g[l]) * (1.0 + scale[:, None, :]) + shift[:, None, :]
        proj = h @ w_in[l]
        q, k, v, g_attn, u, g_ssm = jnp.split(proj, IN_SPLITS, axis=-1)

        q = q.reshape(b_, s_, ATTN_HEADS, 2, HEAD_DIM).transpose(0, 2, 3, 1, 4)
        k = k.reshape(b_, s_, ATTN_HEADS, 2, HEAD_DIM).transpose(0, 2, 3, 1, 4)
        v = v.reshape(b_, s_, ATTN_HEADS, V_HEAD_DIM).transpose(0, 2, 1, 3)
        q = rope(rms_norm(q, q_norm_g[l]), cos, sin)
        k = rope(rms_norm(k, k_norm_g[l]), cos, sin)
        lam = (jnp.exp(jnp.sum(lam_q1[l].astype(f32) * lam_k1[l].astype(f32)))
               - jnp.exp(jnp.sum(lam_q2[l].astype(f32) * lam_k2[l].astype(f32))) + lam_init)
        o = diff_attention(q, k, v, lam)
        o = rms_norm(o, head_norm_g[l]) * (1.0 - lam_init)
        o = o.reshape(b_, s_, ATTN_WIDTH) * jax.nn.silu(g_attn)

        y = s5_ssm(u, ssm_a_re[l], ssm_a_im[l], ssm_log_dt[l], ssm_b_re[l], ssm_b_im[l],
                   ssm_c_re[l], ssm_c_im[l], ssm_d[l])
        z = jax.nn.gelu(y)
        z = z * jax.nn.sigmoid(z @ w_glu[l] + b_glu[l])
        z = z * jax.nn.silu(g_ssm)

        mixed = jnp.concatenate([o, z], axis=-1) @ w_out[l]
        x = x + gate[:, None, :] * mixed
    return x
```

```python
import functools
import math

import jax
import jax.numpy as jnp
from jax import lax
from jax.experimental import pallas as pl
from jax.experimental.pallas import tpu as pltpu

F32 = jnp.float32
BF16 = jnp.bfloat16

D_MODEL = 1024
HEAD_DIM = 64
HALF = HEAD_DIM // 2
V_HEAD = 2 * HEAD_DIM
N_HEADS = D_MODEL // V_HEAD
SSM_GROUPS = 64
SSM_GROUP = 16
SSM_STATE = 64
N_SECTIONS = 6
ROPE_THETA = 10000.0
NORM_EPS = 1e-6
MASK_VALUE = -1e30
CAUSAL_CHUNK_SHIFT = 6
LANES = 128
SSM_CHUNK = LANES
VMEM_LIMIT = 56 * 1024 * 1024


def _silu(v):
    return v * jax.nn.sigmoid(v)


def _ada_kernel(c_ref, w_ref, b_ref, o_ref):
    c = c_ref[...]
    o_ref[...] = jnp.dot(_silu(c), w_ref[...], preferred_element_type=F32) + b_ref[...]


def _ada_mod(c, w_ada, b_ada):
    b = c.shape[0]
    return pl.pallas_call(
        _ada_kernel,
        grid=(3,),
        in_specs=[pl.BlockSpec((b, D_MODEL), lambda j: (0, 0)),
                  pl.BlockSpec((D_MODEL, D_MODEL), lambda j: (0, j)),
                  pl.BlockSpec((1, D_MODEL), lambda j: (0, j))],
        out_specs=pl.BlockSpec((b, D_MODEL), lambda j: (0, j)),
        out_shape=jax.ShapeDtypeStruct((b, 3 * D_MODEL), F32),
        name="ada_mod",
    )(c, w_ada, b_ada.reshape(1, 3 * D_MODEL))


def _inproj_kernel(x_ref, shift_ref, scale_ref, ng_ref, wt_ref, cos_ref, sin_ref, qg_ref, kg_ref,
                   q_ref, k_ref, v_ref, ga_ref, u_ref, gs_ref, *, q_mult):
    tm = x_ref.shape[1]
    x = x_ref[0]
    ms = jnp.mean(x * x, axis=-1, keepdims=True)
    y = x * lax.rsqrt(ms + NORM_EPS) * ng_ref[...]
    h = (y * (1.0 + scale_ref[0]) + shift_ref[0]).astype(BF16)
    cos = cos_ref[...][None]
    sin = sin_ref[...][None]

    def proj(sec):
        w = wt_ref[sec * D_MODEL:(sec + 1) * D_MODEL, :]
        return lax.dot_general(w, h, (((1,), (1,)), ((), ())), preferred_element_type=F32)

    def norm_rope(r, g_ref):
        r3 = r.reshape(2 * N_HEADS, HEAD_DIM, tm)
        ss = jnp.mean(r3 * r3, axis=1, keepdims=True)
        yn = r3 * lax.rsqrt(ss + NORM_EPS) * g_ref[...][None]
        x1 = yn[:, :HALF, :]
        x2 = yn[:, HALF:, :]
        o = jnp.concatenate([x1 * cos - x2 * sin, x2 * cos + x1 * sin], axis=1)
        return o.reshape(D_MODEL, tm)

    q_ref[0] = (norm_rope(proj(0), qg_ref) * q_mult).astype(BF16)
    k_ref[0] = norm_rope(proj(1), kg_ref).T.astype(BF16)
    v_ref[0] = proj(2).astype(BF16)
    ga_ref[0] = _silu(proj(3)).astype(BF16)
    u_ref[...] = proj(4).astype(BF16)
    gs_ref[...] = _silu(proj(5)).astype(BF16)


def _in_proj(x, shift, scale, norm_g, w_in_t, cos_t, sin_t, q_g, k_g, *, tm):
    b, s, _ = x.shape
    nt = s // tm
    q_mult = HEAD_DIM ** -0.5 * math.log2(math.e)
    tok_major = pl.BlockSpec((1, tm, D_MODEL), lambda bi, i: (bi, i, 0))
    ch_major = pl.BlockSpec((1, D_MODEL, tm), lambda bi, i: (bi, 0, i))
    flat_ch_major = pl.BlockSpec((D_MODEL, tm), lambda bi, i: (0, bi * nt + i))
    per_batch = pl.BlockSpec((1, 1, D_MODEL), lambda bi, i: (bi, 0, 0))
    rope_tab = pl.BlockSpec((HALF, tm), lambda bi, i: (0, i))
    col64 = pl.BlockSpec((HEAD_DIM, 1), lambda bi, i: (0, 0))
    return pl.pallas_call(
        functools.partial(_inproj_kernel, q_mult=q_mult),
        grid=(b, nt),
        in_specs=[tok_major, per_batch, per_batch,
                  pl.BlockSpec((1, D_MODEL), lambda bi, i: (0, 0)),
                  pl.BlockSpec((N_SECTIONS * D_MODEL, D_MODEL), lambda bi, i: (0, 0)),
                  rope_tab, rope_tab, col64, col64],
        out_specs=[ch_major, tok_major, ch_major, ch_major, flat_ch_major, flat_ch_major],
        out_shape=[jax.ShapeDtypeStruct((b, D_MODEL, s), BF16),
                   jax.ShapeDtypeStruct((b, s, D_MODEL), BF16),
                   jax.ShapeDtypeStruct((b, D_MODEL, s), BF16),
                   jax.ShapeDtypeStruct((b, D_MODEL, s), BF16),
                   jax.ShapeDtypeStruct((D_MODEL, b * s), BF16),
                   jax.ShapeDtypeStruct((D_MODEL, b * s), BF16)],
        compiler_params=pltpu.CompilerParams(
            dimension_semantics=("parallel", "parallel"), vmem_limit_bytes=VMEM_LIMIT),
        name="in_proj",
    )(x, shift, scale, norm_g, w_in_t, cos_t, sin_t, q_g, k_g)


def _attn_kernel(lam_ref, hg_ref, q_ref, k_ref, v_ref, ga_ref, o_ref,
                 acc1, acc2, m1, l1, m2, l2, *, blk, lam_init):
    qi = pl.program_id(2)
    qt = q_ref[0]
    row = lax.broadcasted_iota(jnp.int32, qt.shape, 0)
    zero = jnp.zeros_like(qt)
    q_maps = (jnp.where(row < HEAD_DIM, qt, zero), jnp.where(row >= HEAD_DIM, qt, zero))
    state = ((acc1, m1, l1), (acc2, m2, l2))
    for acc, m, l in state:
        acc[...] = jnp.zeros_like(acc)
        m[...] = jnp.full_like(m, MASK_VALUE)
        l[...] = jnp.zeros_like(l)

    kv_chunk = lax.broadcasted_iota(jnp.int32, (blk, blk), 0) >> CAUSAL_CHUNK_SHIFT
    q_chunk = lax.broadcasted_iota(jnp.int32, (blk, blk), 1) >> CAUSAL_CHUNK_SHIFT
    allowed = kv_chunk <= q_chunk

    def step(j, masked):
        off = pl.multiple_of(j * blk, blk)
        kt = k_ref[0, pl.ds(off, blk), :]
        vt = v_ref[0, :, pl.ds(off, blk)]
        for qm, (acc, m, l) in zip(q_maps, state):
            s = jnp.dot(kt, qm, preferred_element_type=F32)
            if masked:
                s = jnp.where(allowed, s, MASK_VALUE)
            m_old = m[...]
            m_new = jnp.maximum(m_old, jnp.max(s, axis=0, keepdims=True))
            alpha = jnp.exp2(m_old - m_new)
            p = jnp.exp2(s - m_new)
            l[...] = alpha * l[...] + jnp.sum(p, axis=0, keepdims=True)
            acc[...] = alpha * acc[...] + jnp.dot(vt, p.astype(BF16), preferred_element_type=F32)
            m[...] = m_new

    def body(j, carry):
        step(j, False)
        return carry

    lax.fori_loop(0, qi, body, 0)
    step(qi, True)

    lam_v = lam_ref[...]
    e1 = jnp.exp(jnp.sum(lam_v[0:1] * lam_v[1:2], axis=1, keepdims=True))
    e2 = jnp.exp(jnp.sum(lam_v[2:3] * lam_v[3:4], axis=1, keepdims=True))
    lam = e1 - e2 + lam_init
    o = acc1[...] / l1[...] - lam * (acc2[...] / l2[...])
    ms = jnp.mean(o * o, axis=0, keepdims=True)
    o = o * lax.rsqrt(ms + NORM_EPS) * hg_ref[...] * (1.0 - lam_init)
    o_ref[0] = (o * ga_ref[0].astype(F32)).astype(BF16)


def _diff_attention(lam_vecs, head_g, q_t, k, v_t, ga_t, *, blk, lam_init):
    b, _, s = q_t.shape
    nq = s // blk
    head_tile = pl.BlockSpec((1, V_HEAD, blk), lambda bi, h, i: (bi, h, i))
    return pl.pallas_call(
        functools.partial(_attn_kernel, blk=blk, lam_init=lam_init),
        grid=(b, N_HEADS, nq),
        in_specs=[pl.BlockSpec((4, HEAD_DIM), lambda bi, h, i: (0, 0)),
                  pl.BlockSpec((V_HEAD, 1), lambda bi, h, i: (0, 0)),
                  head_tile,
                  pl.BlockSpec((1, s, V_HEAD), lambda bi, h, i: (bi, 0, h)),
                  pl.BlockSpec((1, V_HEAD, s), lambda bi, h, i: (bi, h, 0)),
                  head_tile],
        out_specs=head_tile,
        out_shape=jax.ShapeDtypeStruct((b, D_MODEL, s), BF16),
        scratch_shapes=[pltpu.VMEM((V_HEAD, blk), F32), pltpu.VMEM((V_HEAD, blk), F32),
                        pltpu.VMEM((1, blk), F32), pltpu.VMEM((1, blk), F32),
                        pltpu.VMEM((1, blk), F32), pltpu.VMEM((1, blk), F32)],
        compiler_params=pltpu.CompilerParams(
            dimension_semantics=("parallel", "parallel", "arbitrary"), vmem_limit_bytes=VMEM_LIMIT),
        name="diff_attn",
    )(lam_vecs, head_g, q_t, k, v_t, ga_t)


def _zoh(a_re, a_im, log_dt):
    dt = jnp.exp(log_dt)
    mag = jnp.exp(a_re * dt)
    lre = mag * jnp.cos(a_im * dt)
    lim = mag * jnp.sin(a_im * dt)
    nr = lre - 1.0
    den = a_re * a_re + a_im * a_im
    fre = (nr * a_re + lim * a_im) / den
    fim = (lim * a_re - nr * a_im) / den
    return lre, lim, fre, fim


def _cpow(bre, bim, e, nbits):
    bre = jnp.broadcast_to(bre, e.shape)
    bim = jnp.broadcast_to(bim, e.shape)
    rre = jnp.ones(e.shape, F32)
    rim = jnp.zeros(e.shape, F32)
    for bit in range(nbits):
        take = ((e >> bit) & 1) == 1
        rre, rim = (jnp.where(take, rre * bre - rim * bim, rre),
                    jnp.where(take, rre * bim + rim * bre, rim))
        bre, bim = bre * bre - bim * bim, 2.0 * bre * bim
    return rre, rim


def _ssm_kernel(u_ref, prow_ref, pcol_ref, bre_ref, bim_ref, x1_ref, x2_ref, cre_ref, cim_ref,
                uu_ref, vv_ref, d_ref, z_ref, kt_ref, pq_ref, y_ref, st_ref, *, n_chunks):
    nbc = u_ref.shape[2]
    n_log = SSM_CHUNK.bit_length() - 1
    prow = prow_ref[0]
    lre, lim, fre, fim = _zoh(prow[0:1], prow[1:2], prow[2:3])
    pcol = pcol_ref[0]
    lre_c, lim_c, _, _ = _zoh(pcol[:, 0:1], pcol[:, 1:2], pcol[:, 2:3])

    lane = lax.broadcasted_iota(jnp.int32, (LANES, LANES), 1)
    sub = lax.broadcasted_iota(jnp.int32, (LANES, LANES), 0)
    ltre, ltim = _cpow(lre_c, lim_c, lane, n_log)
    lt1re = ltre * lre_c - ltim * lim_c
    lt1im = ltre * lim_c + ltim * lre_c
    lsre, lsim = _cpow(lre, lim, (SSM_CHUNK - 1) - sub, n_log)

    fre_h = fre[:, :SSM_STATE]
    fim_h = fim[:, :SSM_STATE]
    bbre = fre_h * bre_ref[0] - fim_h * bim_ref[0]
    bbim = fre_h * bim_ref[0] + fim_h * bre_ref[0]
    pair = (SSM_GROUP, SSM_GROUP, SSM_STATE)
    flat = (SSM_GROUP * SSM_GROUP, SSM_STATE)
    c_re = jnp.broadcast_to(cre_ref[0][:, None, :], pair).reshape(flat)
    c_im = jnp.broadcast_to(cim_ref[0][:, None, :], pair).reshape(flat)
    b_re = jnp.broadcast_to(bbre[None], pair).reshape(flat)
    b_im = jnp.broadcast_to(bbim[None], pair).reshape(flat)
    mre = c_re * b_re - c_im * b_im
    mim = c_re * b_im + c_im * b_re
    kt_ref[...] = (jnp.dot(mre, ltre[:SSM_STATE], preferred_element_type=F32, precision=lax.Precision.HIGHEST)
                   - jnp.dot(mim, ltim[:SSM_STATE], preferred_element_type=F32, precision=lax.Precision.HIGHEST))

    pq_ref[0] = fre * x1_ref[0] + fim * x2_ref[0]
    pq_ref[1] = fre * x2_ref[0] - fim * x1_ref[0]

    causal = lane >= sub
    y_ref[...] = jnp.zeros_like(y_ref)
    st_ref[...] = jnp.zeros_like(st_ref)

    def pair_body(i, carry):
        t_rows, b_rows, xs = [], [], []
        for dh in range(2):
            hi = 2 * i + dh
            blocks = []
            for ho in range(SSM_GROUP):
                lag = jnp.broadcast_to(kt_ref[pl.ds(ho * SSM_GROUP + hi, 1), :], (LANES, LANES))
                toep = pltpu.roll(lag, 0, 1, stride=1, stride_axis=0)
                blocks.append(jnp.where(causal, toep, 0.0).astype(BF16))
            t_rows.append(jnp.concatenate(blocks, axis=1))
            b_rows.append((lsre * pq_ref[0, pl.ds(hi, 1), :] + lsim * pq_ref[1, pl.ds(hi, 1), :]).astype(BF16))
            xs.append(u_ref[0, hi])
        x = jnp.concatenate(xs, axis=1)
        y_ref[...] += jnp.dot(x, jnp.concatenate(t_rows, axis=0), preferred_element_type=F32)
        st_ref[...] += jnp.dot(x, jnp.concatenate(b_rows, axis=0), preferred_element_type=F32)
        return carry

    lax.fori_loop(0, SSM_GROUP // 2, pair_body, 0)

    are, aim = lre, lim
    for _ in range(n_log):
        are, aim = are * are - aim * aim, 2.0 * are * aim
    half_sign = jnp.where(lax.broadcasted_iota(jnp.int32, (1, LANES), 1) < SSM_STATE, -1.0, 1.0)
    chunk_id = lax.broadcasted_iota(jnp.int32, (nbc, LANES), 0) % n_chunks
    acc = st_ref[...]
    dist = 1
    while dist < n_chunks:
        prev = jnp.where(chunk_id >= dist, pltpu.roll(acc, dist, 0), 0.0)
        acc = acc + are * prev + (aim * half_sign) * pltpu.roll(prev, SSM_STATE, 1)
        are, aim = are * are - aim * aim, 2.0 * are * aim
        dist *= 2
    h_in = jnp.where(chunk_id >= 1, pltpu.roll(acc, 1, 0), 0.0).astype(BF16)

    cm = [(lt1re * uu_ref[0][:, ho:ho + 1] + lt1im * vv_ref[0][:, ho:ho + 1]).astype(BF16)
          for ho in range(SSM_GROUP)]
    y_ref[...] += jnp.dot(h_in, jnp.concatenate(cm, axis=1), preferred_element_type=F32)

    for h in range(SSM_GROUP):
        y = y_ref[:, h * LANES:(h + 1) * LANES] + d_ref[0, h] * u_ref[0, h].astype(F32)
        z_ref[0, h] = jax.nn.gelu(y).astype(BF16)


def _s5_groups(u_t, a_re, a_im, log_dt, b_re, b_im, c_re, c_im, d_skip, *, n_chunks):
    nbc = u_t.shape[1] // SSM_CHUNK
    u4 = u_t.reshape(SSM_GROUPS, SSM_GROUP, nbc, SSM_CHUNK)
    g = SSM_GROUPS
    dup = lambda v: jnp.concatenate([v, v], axis=-1)
    ldt = jnp.broadcast_to(log_dt[:, None], (g, LANES))
    prow = jnp.concatenate([jnp.stack([dup(a_re), dup(a_im), ldt], axis=1),
                            jnp.zeros((g, 5, LANES), F32)], axis=1)
    pcol = prow.transpose(0, 2, 1)
    bre_t = b_re.transpose(0, 2, 1)
    bim_t = b_im.transpose(0, 2, 1)
    x1 = jnp.concatenate([bre_t, bim_t], axis=-1)
    x2 = jnp.concatenate([-bim_t, bre_t], axis=-1)
    cre_t = c_re.transpose(0, 2, 1)
    cim_t = c_im.transpose(0, 2, 1)
    uu = jnp.concatenate([cre_t, -cim_t], axis=1)
    vv = jnp.concatenate([-cim_t, -cre_t], axis=1)
    d_b = jnp.broadcast_to(d_skip.reshape(g, SSM_GROUP, 1, 1), (g, SSM_GROUP, 1, LANES))

    def per_group(*tail):
        return pl.BlockSpec((1,) + tail, lambda gi: (gi,) + (0,) * len(tail))

    z4 = pl.pallas_call(
        functools.partial(_ssm_kernel, n_chunks=n_chunks),
        grid=(g,),
        in_specs=[per_group(SSM_GROUP, nbc, SSM_CHUNK), per_group(8, LANES), per_group(LANES, 8),
                  per_group(SSM_GROUP, SSM_STATE), per_group(SSM_GROUP, SSM_STATE),
                  per_group(SSM_GROUP, LANES), per_group(SSM_GROUP, LANES),
                  per_group(SSM_GROUP, SSM_STATE), per_group(SSM_GROUP, SSM_STATE),
                  per_group(LANES, SSM_GROUP), per_group(LANES, SSM_GROUP),
                  per_group(SSM_GROUP, 1, LANES)],
        out_specs=per_group(SSM_GROUP, nbc, SSM_CHUNK),
        out_shape=jax.ShapeDtypeStruct(u4.shape, BF16),
        scratch_shapes=[pltpu.VMEM((SSM_GROUP * SSM_GROUP, LANES), F32),
                        pltpu.VMEM((2, SSM_GROUP, LANES), F32),
                        pltpu.VMEM((nbc, SSM_GROUP * SSM_CHUNK), F32),
                        pltpu.VMEM((nbc, LANES), F32)],
        compiler_params=pltpu.CompilerParams(
            dimension_semantics=("parallel",), vmem_limit_bytes=VMEM_LIMIT),
        name="s5_groups",
    )(u4, prow, pcol, bre_t, bim_t, x1, x2, c_re, c_im, uu, vv, d_b)
    return z4.reshape(u_t.shape)


def _out_kernel(o_ref, z_ref, gs_ref, x_ref, gate_ref, wg_ref, bg_ref, wo_ref, out_ref):
    z0 = z_ref[...]
    t = jnp.dot(wg_ref[...], z0, preferred_element_type=F32) + bg_ref[...]
    z = z0.astype(F32) * jax.nn.sigmoid(t) * gs_ref[...].astype(F32)
    mixed_t = (jnp.dot(wo_ref[:, :D_MODEL], o_ref[0], preferred_element_type=F32)
               + jnp.dot(wo_ref[:, D_MODEL:], z.astype(BF16), preferred_element_type=F32))
    out_ref[0] = x_ref[0] + gate_ref[0] * mixed_t.T


def _mix_out(o_t, z_t, gs_t, x, gate, w_glu_t, b_glu, w_out_t, *, tn):
    b, s, _ = x.shape
    nt = s // tn
    flat_ch_major = pl.BlockSpec((D_MODEL, tn), lambda bi, i: (0, bi * nt + i))
    tok_major = pl.BlockSpec((1, tn, D_MODEL), lambda bi, i: (bi, i, 0))
    return pl.pallas_call(
        _out_kernel,
        grid=(b, nt),
        in_specs=[pl.BlockSpec((1, D_MODEL, tn), lambda bi, i: (bi, 0, i)),
                  flat_ch_major, flat_ch_major, tok_major,
                  pl.BlockSpec((1, 1, D_MODEL), lambda bi, i: (bi, 0, 0)),
                  pl.BlockSpec((D_MODEL, D_MODEL), lambda bi, i: (0, 0)),
                  pl.BlockSpec((D_MODEL, 1), lambda bi, i: (0, 0)),
                  pl.BlockSpec((D_MODEL, 2 * D_MODEL), lambda bi, i: (0, 0))],
        out_specs=tok_major,
        out_shape=jax.ShapeDtypeStruct(x.shape, x.dtype),
        compiler_params=pltpu.CompilerParams(
            dimension_semantics=("parallel", "parallel"), vmem_limit_bytes=VMEM_LIMIT),
        name="mix_out",
    )(o_t, z_t, gs_t, x, gate, w_glu_t, b_glu, w_out_t)


def kernel(x, c, w_ada, b_ada, norm_g, w_in, q_norm_g, k_norm_g, lam_q1, lam_k1, lam_q2, lam_k2,
           head_norm_g, ssm_a_re, ssm_a_im, ssm_log_dt, ssm_b_re, ssm_b_im, ssm_c_re, ssm_c_im,
           ssm_d, w_glu, b_glu, w_out):
    b, s, _ = x.shape
    depth = w_in.shape[0]
    tok_tile = min(512, s)
    attn_tile = min(512, s)
    assert s % tok_tile == 0 and s % attn_tile == 0 and s % SSM_CHUNK == 0 and tok_tile % LANES == 0

    pos = jnp.arange(s, dtype=F32)
    inv_freq = 1.0 / (ROPE_THETA ** (jnp.arange(0, HEAD_DIM, 2, dtype=F32) / HEAD_DIM))
    ang_t = inv_freq[:, None] * pos[None, :]
    cos_t, sin_t = jnp.cos(ang_t), jnp.sin(ang_t)

    for l in range(depth):
        lam_init = 0.8 - 0.6 * math.exp(-0.3 * l)
        mod = _ada_mod(c, w_ada[l], b_ada[l])
        shift, scale, gate = (mod[:, i * D_MODEL:(i + 1) * D_MODEL].reshape(b, 1, D_MODEL) for i in range(3))
        q_t, k, v_t, ga_t, u_t, gs_t = _in_proj(
            x, shift, scale, norm_g[l].reshape(1, D_MODEL), w_in[l].T.astype(BF16), cos_t, sin_t,
            q_norm_g[l].reshape(HEAD_DIM, 1), k_norm_g[l].reshape(HEAD_DIM, 1), tm=tok_tile)
        lam_vecs = jnp.stack([lam_q1[l], lam_k1[l], lam_q2[l], lam_k2[l]]).astype(F32)
        o_t = _diff_attention(lam_vecs, head_norm_g[l].reshape(V_HEAD, 1), q_t, k, v_t, ga_t,
                              blk=attn_tile, lam_init=lam_init)
        z_t = _s5_groups(u_t, ssm_a_re[l], ssm_a_im[l], ssm_log_dt[l], ssm_b_re[l], ssm_b_im[l],
                         ssm_c_re[l], ssm_c_im[l], ssm_d[l], n_chunks=s // SSM_CHUNK)
        x = _mix_out(o_t, z_t, gs_t, x, gate, w_glu[l].T.astype(BF16), b_glu[l].reshape(D_MODEL, 1),
                     w_out[l].T.astype(BF16), tn=tok_tile)
    return x
```

```python
import functools
import math

import jax
import jax.numpy as jnp
from jax import lax
from jax.experimental import pallas as pl
from jax.experimental.pallas import tpu as pltpu

F32 = jnp.float32
BF16 = jnp.bfloat16

D_MODEL = 1024
HEAD_DIM = 64
HALF = HEAD_DIM // 2
V_HEAD = 2 * HEAD_DIM
N_HEADS = D_MODEL // V_HEAD
SSM_GROUPS = 64
SSM_GROUP = 16
SSM_STATE = 64
N_SECTIONS = 6
ROPE_THETA = 10000.0
NORM_EPS = 1e-6
MASK_VALUE = -1e30
CAUSAL_CHUNK_SHIFT = 6
LANES = 128
SSM_CHUNK = LANES
VMEM_LIMIT = 56 * 1024 * 1024
ATTN_HEADS_PER_STEP = 2


def _silu(v):
    return v * jax.nn.sigmoid(v)


def _ada_kernel(c_ref, w_ref, b_ref, o_ref):
    c = c_ref[...]
    o_ref[...] = jnp.dot(_silu(c), w_ref[...], preferred_element_type=F32) + b_ref[...]


def _ada_mod(c, w_ada, b_ada):
    b = c.shape[0]
    return pl.pallas_call(
        _ada_kernel,
        grid=(3,),
        in_specs=[pl.BlockSpec((b, D_MODEL), lambda j: (0, 0)),
                  pl.BlockSpec((D_MODEL, D_MODEL), lambda j: (0, j)),
                  pl.BlockSpec((1, D_MODEL), lambda j: (0, j))],
        out_specs=pl.BlockSpec((b, D_MODEL), lambda j: (0, j)),
        out_shape=jax.ShapeDtypeStruct((b, 3 * D_MODEL), F32),
        name="ada_mod",
    )(c, w_ada, b_ada.reshape(1, 3 * D_MODEL))


def _inproj_kernel(x_ref, shift_ref, scale_ref, ng_ref, wt_ref, cos_ref, sin_ref, qg_ref, kg_ref,
                   q_ref, k_ref, v_ref, ga_ref, u_ref, gs_ref, *, q_mult):
    tm = x_ref.shape[1]
    x = x_ref[0]
    ms = jnp.mean(x * x, axis=-1, keepdims=True)
    y = x * lax.rsqrt(ms + NORM_EPS) * ng_ref[...]
    h = (y * (1.0 + scale_ref[0]) + shift_ref[0]).astype(BF16)
    cos = cos_ref[...][None]
    sin = sin_ref[...][None]

    def proj(sec):
        w = wt_ref[sec * D_MODEL:(sec + 1) * D_MODEL, :]
        return lax.dot_general(w, h, (((1,), (1,)), ((), ())), preferred_element_type=F32)

    def norm_rope(r, g_ref):
        r3 = r.reshape(2 * N_HEADS, HEAD_DIM, tm)
        ss = jnp.mean(r3 * r3, axis=1, keepdims=True)
        yn = r3 * lax.rsqrt(ss + NORM_EPS) * g_ref[...][None]
        x1 = yn[:, :HALF, :]
        x2 = yn[:, HALF:, :]
        o = jnp.concatenate([x1 * cos - x2 * sin, x2 * cos + x1 * sin], axis=1)
        return o.reshape(D_MODEL, tm)

    q_ref[0] = (norm_rope(proj(0), qg_ref) * q_mult).astype(BF16)
    k_ref[0] = norm_rope(proj(1), kg_ref).T.astype(BF16)
    v_ref[0] = proj(2).astype(BF16)
    ga_ref[0] = _silu(proj(3)).astype(BF16)
    u_ref[...] = proj(4).astype(BF16)
    gs_ref[...] = _silu(proj(5)).astype(BF16)


def _in_proj(x, shift, scale, norm_g, w_in_t, cos_t, sin_t, q_g, k_g, *, tm):
    b, s, _ = x.shape
    nt = s // tm
    q_mult = HEAD_DIM ** -0.5 * math.log2(math.e)
    tok_major = pl.BlockSpec((1, tm, D_MODEL), lambda bi, i: (bi, i, 0))
    ch_major = pl.BlockSpec((1, D_MODEL, tm), lambda bi, i: (bi, 0, i))
    flat_ch_major = pl.BlockSpec((D_MODEL, tm), lambda bi, i: (0, bi * nt + i))
    per_batch = pl.BlockSpec((1, 1, D_MODEL), lambda bi, i: (bi, 0, 0))
    rope_tab = pl.BlockSpec((HALF, tm), lambda bi, i: (0, i))
    col64 = pl.BlockSpec((HEAD_DIM, 1), lambda bi, i: (0, 0))
    return pl.pallas_call(
        functools.partial(_inproj_kernel, q_mult=q_mult),
        grid=(b, nt),
        in_specs=[tok_major, per_batch, per_batch,
                  pl.BlockSpec((1, D_MODEL), lambda bi, i: (0, 0)),
                  pl.BlockSpec((N_SECTIONS * D_MODEL, D_MODEL), lambda bi, i: (0, 0)),
                  rope_tab, rope_tab, col64, col64],
        out_specs=[ch_major, tok_major, ch_major, ch_major, flat_ch_major, flat_ch_major],
        out_shape=[jax.ShapeDtypeStruct((b, D_MODEL, s), BF16),
                   jax.ShapeDtypeStruct((b, s, D_MODEL), BF16),
                   jax.ShapeDtypeStruct((b, D_MODEL, s), BF16),
                   jax.ShapeDtypeStruct((b, D_MODEL, s), BF16),
                   jax.ShapeDtypeStruct((D_MODEL, b * s), BF16),
                   jax.ShapeDtypeStruct((D_MODEL, b * s), BF16)],
        compiler_params=pltpu.CompilerParams(
            dimension_semantics=("parallel", "parallel"), vmem_limit_bytes=VMEM_LIMIT),
        name="in_proj",
    )(x, shift, scale, norm_g, w_in_t, cos_t, sin_t, q_g, k_g)


def _attn_kernel(lam_ref, hg_ref, q_ref, k_ref, v_ref, ga_ref, o_ref,
                 acc_scr, m_scr, l_scr, s_scr, c_scr, *, blk, heads, lam_init):
    qi = pl.program_id(2)
    n_streams = 2 * heads
    row = lax.broadcasted_iota(jnp.int32, (V_HEAD, blk), 0)
    q_maps = []
    for h in range(heads):
        qt = q_ref[0, h * V_HEAD:(h + 1) * V_HEAD, :]
        zero = jnp.zeros_like(qt)
        q_maps += [jnp.where(row < HEAD_DIM, qt, zero), jnp.where(row >= HEAD_DIM, qt, zero)]
    acc_scr[...] = jnp.zeros_like(acc_scr)
    m_scr[...] = jnp.full_like(m_scr, MASK_VALUE)
    l_scr[...] = jnp.zeros_like(l_scr)

    kv_chunk = lax.broadcasted_iota(jnp.int32, (blk, blk), 0) >> CAUSAL_CHUNK_SHIFT
    q_chunk = lax.broadcasted_iota(jnp.int32, (blk, blk), 1) >> CAUSAL_CHUNK_SHIFT

    def scores(j, allowed):
        rows = pl.ds(pl.multiple_of(j * blk, blk), blk)
        for n in range(n_streams):
            h = n // 2
            kt = k_ref[0, rows, h * V_HEAD:(h + 1) * V_HEAD]
            s = jnp.dot(kt, q_maps[n], preferred_element_type=F32)
            if allowed is not None:
                s = jnp.where(allowed, s, MASK_VALUE)
            s_scr[n] = s
            c_scr[n] = jnp.max(s, axis=0, keepdims=True)

    def probs():
        out = []
        for n in range(n_streams):
            m_old = m_scr[n]
            m_new = jnp.maximum(m_old, c_scr[n])
            alpha = jnp.exp2(m_old - m_new)
            p = jnp.exp2(s_scr[n] - m_new)
            l_scr[n] = alpha * l_scr[n] + jnp.sum(p, axis=0, keepdims=True)
            m_scr[n] = m_new
            out.append((alpha, p.astype(BF16)))
        return out

    def accumulate(j, alpha_p):
        cols = pl.ds(pl.multiple_of(j * blk, blk), blk)
        for n, (alpha, p) in enumerate(alpha_p):
            h = n // 2
            vt = v_ref[0, h * V_HEAD:(h + 1) * V_HEAD, cols]
            acc_scr[n] = alpha * acc_scr[n] + jnp.dot(vt, p, preferred_element_type=F32)

    def step(j, next_allowed):
        alpha_p = probs()
        scores(j + 1, next_allowed)
        accumulate(j, alpha_p)

    scores(0, kv_chunk <= q_chunk + qi * (blk >> CAUSAL_CHUNK_SHIFT))

    def body(j, carry):
        step(j, None)
        return carry

    lax.fori_loop(0, qi - 1, body, 0)

    @pl.when(qi >= 1)
    def _():
        step(qi - 1, kv_chunk <= q_chunk)

    accumulate(qi, probs())

    lam_v = lam_ref[...]
    e1 = jnp.exp(jnp.sum(lam_v[0:1] * lam_v[1:2], axis=1, keepdims=True))
    e2 = jnp.exp(jnp.sum(lam_v[2:3] * lam_v[3:4], axis=1, keepdims=True))
    lam = e1 - e2 + lam_init
    for h in range(heads):
        o = (acc_scr[2 * h] / l_scr[2 * h]
             - lam * (acc_scr[2 * h + 1] / l_scr[2 * h + 1]))
        ms = jnp.mean(o * o, axis=0, keepdims=True)
        o = o * lax.rsqrt(ms + NORM_EPS) * hg_ref[...] * (1.0 - lam_init)
        head_rows = slice(h * V_HEAD, (h + 1) * V_HEAD)
        o_ref[0, head_rows, :] = (o * ga_ref[0, head_rows, :].astype(F32)).astype(BF16)


def _diff_attention(lam_vecs, head_g, q_t, k, v_t, ga_t, *, blk, heads, lam_init):
    b, _, s = q_t.shape
    nq = s // blk
    n_streams = 2 * heads
    width = heads * V_HEAD
    head_tile = pl.BlockSpec((1, width, blk), lambda bi, h, i: (bi, h, i))
    return pl.pallas_call(
        functools.partial(_attn_kernel, blk=blk, heads=heads, lam_init=lam_init),
        grid=(b, N_HEADS // heads, nq),
        in_specs=[pl.BlockSpec((4, HEAD_DIM), lambda bi, h, i: (0, 0)),
                  pl.BlockSpec((V_HEAD, 1), lambda bi, h, i: (0, 0)),
                  head_tile,
                  pl.BlockSpec((1, s, width), lambda bi, h, i: (bi, 0, h)),
                  pl.BlockSpec((1, width, s), lambda bi, h, i: (bi, h, 0)),
                  head_tile],
        out_specs=head_tile,
        out_shape=jax.ShapeDtypeStruct((b, D_MODEL, s), BF16),
        scratch_shapes=[pltpu.VMEM((n_streams, V_HEAD, blk), F32),
                        pltpu.VMEM((n_streams, 1, blk), F32),
                        pltpu.VMEM((n_streams, 1, blk), F32),
                        pltpu.VMEM((n_streams, blk, blk), F32),
                        pltpu.VMEM((n_streams, 1, blk), F32)],
        compiler_params=pltpu.CompilerParams(
            dimension_semantics=("parallel", "parallel", "arbitrary"), vmem_limit_bytes=VMEM_LIMIT),
        name="diff_attn",
    )(lam_vecs, head_g, q_t, k, v_t, ga_t)


def _zoh(a_re, a_im, log_dt):
    dt = jnp.exp(log_dt)
    mag = jnp.exp(a_re * dt)
    lre = mag * jnp.cos(a_im * dt)
    lim = mag * jnp.sin(a_im * dt)
    nr = lre - 1.0
    den = a_re * a_re + a_im * a_im
    fre = (nr * a_re + lim * a_im) / den
    fim = (lim * a_re - nr * a_im) / den
    return lre, lim, fre, fim


def _cpow(bre, bim, e, nbits):
    bre = jnp.broadcast_to(bre, e.shape)
    bim = jnp.broadcast_to(bim, e.shape)
    rre = jnp.ones(e.shape, F32)
    rim = jnp.zeros(e.shape, F32)
    for bit in range(nbits):
        take = ((e >> bit) & 1) == 1
        rre, rim = (jnp.where(take, rre * bre - rim * bim, rre),
                    jnp.where(take, rre * bim + rim * bre, rim))
        bre, bim = bre * bre - bim * bim, 2.0 * bre * bim
    return rre, rim


def _ssm_kernel(u_ref, prow_ref, pcol_ref, bre_ref, bim_ref, x1_ref, x2_ref, cre_ref, cim_ref,
                uu_ref, vv_ref, d_ref, z_ref, kt_ref, pq_ref, y_ref, st_ref, *, n_chunks):
    nbc = u_ref.shape[2]
    n_log = SSM_CHUNK.bit_length() - 1
    prow = prow_ref[0]
    lre, lim, fre, fim = _zoh(prow[0:1], prow[1:2], prow[2:3])
    pcol = pcol_ref[0]
    lre_c, lim_c, _, _ = _zoh(pcol[:, 0:1], pcol[:, 1:2], pcol[:, 2:3])

    lane = lax.broadcasted_iota(jnp.int32, (LANES, LANES), 1)
    sub = lax.broadcasted_iota(jnp.int32, (LANES, LANES), 0)
    ltre, ltim = _cpow(lre_c, lim_c, lane, n_log)
    lt1re = ltre * lre_c - ltim * lim_c
    lt1im = ltre * lim_c + ltim * lre_c
    lsre, lsim = _cpow(lre, lim, (SSM_CHUNK - 1) - sub, n_log)

    fre_h = fre[:, :SSM_STATE]
    fim_h = fim[:, :SSM_STATE]
    bbre = fre_h * bre_ref[0] - fim_h * bim_ref[0]
    bbim = fre_h * bim_ref[0] + fim_h * bre_ref[0]
    pair = (SSM_GROUP, SSM_GROUP, SSM_STATE)
    flat = (SSM_GROUP * SSM_GROUP, SSM_STATE)
    c_re = jnp.broadcast_to(cre_ref[0][:, None, :], pair).reshape(flat)
    c_im = jnp.broadcast_to(cim_ref[0][:, None, :], pair).reshape(flat)
    b_re = jnp.broadcast_to(bbre[None], pair).reshape(flat)
    b_im = jnp.broadcast_to(bbim[None], pair).reshape(flat)
    mre = c_re * b_re - c_im * b_im
    mim = c_re * b_im + c_im * b_re
    kt_ref[...] = (jnp.dot(mre, ltre[:SSM_STATE], preferred_element_type=F32, precision=lax.Precision.HIGHEST)
                   - jnp.dot(mim, ltim[:SSM_STATE], preferred_element_type=F32, precision=lax.Precision.HIGHEST))

    pq_ref[0] = fre * x1_ref[0] + fim * x2_ref[0]
    pq_ref[1] = fre * x2_ref[0] - fim * x1_ref[0]

    causal = lane >= sub
    y_ref[...] = jnp.zeros_like(y_ref)
    st_ref[...] = jnp.zeros_like(st_ref)

    def pair_body(i, carry):
        t_rows, b_rows, xs = [], [], []
        for dh in range(2):
            hi = 2 * i + dh
            blocks = []
            for ho in range(SSM_GROUP):
                lag = jnp.broadcast_to(kt_ref[pl.ds(ho * SSM_GROUP + hi, 1), :], (LANES, LANES))
                toep = pltpu.roll(lag, 0, 1, stride=1, stride_axis=0)
                blocks.append(jnp.where(causal, toep, 0.0).astype(BF16))
            t_rows.append(jnp.concatenate(blocks, axis=1))
            b_rows.append((lsre * pq_ref[0, pl.ds(hi, 1), :] + lsim * pq_ref[1, pl.ds(hi, 1), :]).astype(BF16))
            xs.append(u_ref[0, hi])
        x = jnp.concatenate(xs, axis=1)
        y_ref[...] += jnp.dot(x, jnp.concatenate(t_rows, axis=0), preferred_element_type=F32)
        st_ref[...] += jnp.dot(x, jnp.concatenate(b_rows, axis=0), preferred_element_type=F32)
        return carry

    lax.fori_loop(0, SSM_GROUP // 2, pair_body, 0)

    are, aim = lre, lim
    for _ in range(n_log):
        are, aim = are * are - aim * aim, 2.0 * are * aim
    half_sign = jnp.where(lax.broadcasted_iota(jnp.int32, (1, LANES), 1) < SSM_STATE, -1.0, 1.0)
    chunk_id = lax.broadcasted_iota(jnp.int32, (nbc, LANES), 0) % n_chunks
    acc = st_ref[...]
    dist = 1
    while dist < n_chunks:
        prev = jnp.where(chunk_id >= dist, pltpu.roll(acc, dist, 0), 0.0)
        acc = acc + are * prev + (aim * half_sign) * pltpu.roll(prev, SSM_STATE, 1)
        are, aim = are * are - aim * aim, 2.0 * are * aim
        dist *= 2
    h_in = jnp.where(chunk_id >= 1, pltpu.roll(acc, 1, 0), 0.0).astype(BF16)

    cm = [(lt1re * uu_ref[0][:, ho:ho + 1] + lt1im * vv_ref[0][:, ho:ho + 1]).astype(BF16)
          for ho in range(SSM_GROUP)]
    y_ref[...] += jnp.dot(h_in, jnp.concatenate(cm, axis=1), preferred_element_type=F32)

    for h in range(SSM_GROUP):
        y = y_ref[:, h * LANES:(h + 1) * LANES] + d_ref[0, h] * u_ref[0, h].astype(F32)
        z_ref[0, h] = jax.nn.gelu(y).astype(BF16)


def _s5_groups(u_t, a_re, a_im, log_dt, b_re, b_im, c_re, c_im, d_skip, *, n_chunks):
    nbc = u_t.shape[1] // SSM_CHUNK
    u4 = u_t.reshape(SSM_GROUPS, SSM_GROUP, nbc, SSM_CHUNK)
    g = SSM_GROUPS
    dup = lambda v: jnp.concatenate([v, v], axis=-1)
    ldt = jnp.broadcast_to(log_dt[:, None], (g, LANES))
    prow = jnp.concatenate([jnp.stack([dup(a_re), dup(a_im), ldt], axis=1),
                            jnp.zeros((g, 5, LANES), F32)], axis=1)
    pcol = prow.transpose(0, 2, 1)
    bre_t = b_re.transpose(0, 2, 1)
    bim_t = b_im.transpose(0, 2, 1)
    x1 = jnp.concatenate([bre_t, bim_t], axis=-1)
    x2 = jnp.concatenate([-bim_t, bre_t], axis=-1)
    cre_t = c_re.transpose(0, 2, 1)
    cim_t = c_im.transpose(0, 2, 1)
    uu = jnp.concatenate([cre_t, -cim_t], axis=1)
    vv = jnp.concatenate([-cim_t, -cre_t], axis=1)
    d_b = jnp.broadcast_to(d_skip.reshape(g, SSM_GROUP, 1, 1), (g, SSM_GROUP, 1, LANES))

    def per_group(*tail):
        return pl.BlockSpec((1,) + tail, lambda gi: (gi,) + (0,) * len(tail))

    z4 = pl.pallas_call(
        functools.partial(_ssm_kernel, n_chunks=n_chunks),
        grid=(g,),
        in_specs=[per_group(SSM_GROUP, nbc, SSM_CHUNK), per_group(8, LANES), per_group(LANES, 8),
                  per_group(SSM_GROUP, SSM_STATE), per_group(SSM_GROUP, SSM_STATE),
                  per_group(SSM_GROUP, LANES), per_group(SSM_GROUP, LANES),
                  per_group(SSM_GROUP, SSM_STATE), per_group(SSM_GROUP, SSM_STATE),
                  per_group(LANES, SSM_GROUP), per_group(LANES, SSM_GROUP),
                  per_group(SSM_GROUP, 1, LANES)],
        out_specs=per_group(SSM_GROUP, nbc, SSM_CHUNK),
        out_shape=jax.ShapeDtypeStruct(u4.shape, BF16),
        scratch_shapes=[pltpu.VMEM((SSM_GROUP * SSM_GROUP, LANES), F32),
                        pltpu.VMEM((2, SSM_GROUP, LANES), F32),
                        pltpu.VMEM((nbc, SSM_GROUP * SSM_CHUNK), F32),
                        pltpu.VMEM((nbc, LANES), F32)],
        compiler_params=pltpu.CompilerParams(
            dimension_semantics=("parallel",), vmem_limit_bytes=VMEM_LIMIT),
        name="s5_groups",
    )(u4, prow, pcol, bre_t, bim_t, x1, x2, c_re, c_im, uu, vv, d_b)
    return z4.reshape(u_t.shape)


def _out_kernel(o_ref, z_ref, gs_ref, x_ref, gate_ref, wg_ref, bg_ref, wo_ref, out_ref):
    z0 = z_ref[...]
    t = jnp.dot(wg_ref[...], z0, preferred_element_type=F32) + bg_ref[...]
    z = z0.astype(F32) * jax.nn.sigmoid(t) * gs_ref[...].astype(F32)
    mixed_t = (jnp.dot(wo_ref[:, :D_MODEL], o_ref[0], preferred_element_type=F32)
               + jnp.dot(wo_ref[:, D_MODEL:], z.astype(BF16), preferred_element_type=F32))
    out_ref[0] = x_ref[0] + gate_ref[0] * mixed_t.T


def _mix_out(o_t, z_t, gs_t, x, gate, w_glu_t, b_glu, w_out_t, *, tn):
    b, s, _ = x.shape
    nt = s // tn
    flat_ch_major = pl.BlockSpec((D_MODEL, tn), lambda bi, i: (0, bi * nt + i))
    tok_major = pl.BlockSpec((1, tn, D_MODEL), lambda bi, i: (bi, i, 0))
    return pl.pallas_call(
        _out_kernel,
        grid=(b, nt),
        in_specs=[pl.BlockSpec((1, D_MODEL, tn), lambda bi, i: (bi, 0, i)),
                  flat_ch_major, flat_ch_major, tok_major,
                  pl.BlockSpec((1, 1, D_MODEL), lambda bi, i: (bi, 0, 0)),
                  pl.BlockSpec((D_MODEL, D_MODEL), lambda bi, i: (0, 0)),
                  pl.BlockSpec((D_MODEL, 1), lambda bi, i: (0, 0)),
                  pl.BlockSpec((D_MODEL, 2 * D_MODEL), lambda bi, i: (0, 0))],
        out_specs=tok_major,
        out_shape=jax.ShapeDtypeStruct(x.shape, x.dtype),
        compiler_params=pltpu.CompilerParams(
            dimension_semantics=("parallel", "parallel"), vmem_limit_bytes=VMEM_LIMIT),
        name="mix_out",
    )(o_t, z_t, gs_t, x, gate, w_glu_t, b_glu, w_out_t)


def kernel(x, c, w_ada, b_ada, norm_g, w_in, q_norm_g, k_norm_g, lam_q1, lam_k1, lam_q2, lam_k2,
           head_norm_g, ssm_a_re, ssm_a_im, ssm_log_dt, ssm_b_re, ssm_b_im, ssm_c_re, ssm_c_im,
           ssm_d, w_glu, b_glu, w_out):
    b, s, _ = x.shape
    depth = w_in.shape[0]
    tok_tile = min(512, s)
    attn_tile = min(512, s)
    assert s % tok_tile == 0 and s % attn_tile == 0 and s % SSM_CHUNK == 0 and tok_tile % LANES == 0

    pos = jnp.arange(s, dtype=F32)
    inv_freq = 1.0 / (ROPE_THETA ** (jnp.arange(0, HEAD_DIM, 2, dtype=F32) / HEAD_DIM))
    ang_t = inv_freq[:, None] * pos[None, :]
    cos_t, sin_t = jnp.cos(ang_t), jnp.sin(ang_t)

    for l in range(depth):
        lam_init = 0.8 - 0.6 * math.exp(-0.3 * l)
        mod = _ada_mod(c, w_ada[l], b_ada[l])
        shift, scale, gate = (mod[:, i * D_MODEL:(i + 1) * D_MODEL].reshape(b, 1, D_MODEL) for i in range(3))
        q_t, k, v_t, ga_t, u_t, gs_t = _in_proj(
            x, shift, scale, norm_g[l].reshape(1, D_MODEL), w_in[l].T.astype(BF16), cos_t, sin_t,
            q_norm_g[l].reshape(HEAD_DIM, 1), k_norm_g[l].reshape(HEAD_DIM, 1), tm=tok_tile)
        lam_vecs = jnp.stack([lam_q1[l], lam_k1[l], lam_q2[l], lam_k2[l]]).astype(F32)
        o_t = _diff_attention(lam_vecs, head_norm_g[l].reshape(V_HEAD, 1), q_t, k, v_t, ga_t,
                              blk=attn_tile, heads=ATTN_HEADS_PER_STEP, lam_init=lam_init)
        z_t = _s5_groups(u_t, ssm_a_re[l], ssm_a_im[l], ssm_log_dt[l], ssm_b_re[l], ssm_b_im[l],
                         ssm_c_re[l], ssm_c_im[l], ssm_d[l], n_chunks=s // SSM_CHUNK)
        x = _mix_out(o_t, z_t, gs_t, x, gate, w_glu[l].T.astype(BF16), b_glu[l].reshape(D_MODEL, 1),
                     w_out[l].T.astype(BF16), tn=tok_tile)
    return x
```

```python
import functools
import math

import jax
import jax.numpy as jnp
from jax import lax
from jax.experimental import pallas as pl
from jax.experimental.pallas import tpu as pltpu

F32 = jnp.float32
BF16 = jnp.bfloat16

D_MODEL = 1024
HEAD_DIM = 64
HALF = HEAD_DIM // 2
V_HEAD = 2 * HEAD_DIM
N_HEADS = D_MODEL // V_HEAD
SSM_GROUPS = 64
SSM_GROUP = 16
SSM_STATE = 64
N_SECTIONS = 6
ROPE_THETA = 10000.0
NORM_EPS = 1e-6
MASK_VALUE = -1e30
CAUSAL_CHUNK_SHIFT = 6
LANES = 128
SSM_CHUNK = LANES
VMEM_LIMIT = 56 * 1024 * 1024
ATTN_HEADS_PER_STEP = 2


def _silu(v):
    return v * jax.nn.sigmoid(v)


def _ada_kernel(c_ref, w_ref, b_ref, o_ref):
    c = c_ref[...]
    o_ref[...] = jnp.dot(_silu(c), w_ref[...], preferred_element_type=F32) + b_ref[...]


def _ada_mod(c, w_ada, b_ada):
    b = c.shape[0]
    return pl.pallas_call(
        _ada_kernel,
        grid=(3,),
        in_specs=[pl.BlockSpec((b, D_MODEL), lambda j: (0, 0)),
                  pl.BlockSpec((D_MODEL, D_MODEL), lambda j: (0, j)),
                  pl.BlockSpec((1, D_MODEL), lambda j: (0, j))],
        out_specs=pl.BlockSpec((b, D_MODEL), lambda j: (0, j)),
        out_shape=jax.ShapeDtypeStruct((b, 3 * D_MODEL), F32),
        name="ada_mod",
    )(c, w_ada, b_ada.reshape(1, 3 * D_MODEL))


def _inproj_kernel(x_ref, shift_ref, scale_ref, ng_ref, wt_ref, cos_ref, sin_ref, qg_ref, kg_ref,
                   q_ref, k_ref, v_ref, ga_ref, u_ref, gs_ref, *, q_mult):
    tm = x_ref.shape[1]
    x = x_ref[0]
    ms = jnp.mean(x * x, axis=-1, keepdims=True)
    y = x * lax.rsqrt(ms + NORM_EPS) * ng_ref[...]
    h = (y * (1.0 + scale_ref[0]) + shift_ref[0]).astype(BF16)
    cos = cos_ref[...][None]
    sin = sin_ref[...][None]

    def proj(sec):
        w = wt_ref[sec * D_MODEL:(sec + 1) * D_MODEL, :]
        return lax.dot_general(w, h, (((1,), (1,)), ((), ())), preferred_element_type=F32)

    def norm_rope(r, g_ref):
        r3 = r.reshape(2 * N_HEADS, HEAD_DIM, tm)
        ss = jnp.mean(r3 * r3, axis=1, keepdims=True)
        yn = r3 * lax.rsqrt(ss + NORM_EPS) * g_ref[...][None]
        x1 = yn[:, :HALF, :]
        x2 = yn[:, HALF:, :]
        o = jnp.concatenate([x1 * cos - x2 * sin, x2 * cos + x1 * sin], axis=1)
        return o.reshape(D_MODEL, tm)

    q_ref[0] = (norm_rope(proj(0), qg_ref) * q_mult).astype(BF16)
    k_ref[0] = norm_rope(proj(1), kg_ref).T.astype(BF16)
    v_ref[0] = proj(2).astype(BF16)
    ga_ref[0] = _silu(proj(3)).astype(BF16)
    u_ref[...] = proj(4).astype(BF16)
    gs_ref[...] = _silu(proj(5)).astype(BF16)


def _in_proj(x, shift, scale, norm_g, w_in_t, cos_t, sin_t, q_g, k_g, *, tm):
    b, s, _ = x.shape
    nt = s // tm
    q_mult = HEAD_DIM ** -0.5 * math.log2(math.e)
    tok_major = pl.BlockSpec((1, tm, D_MODEL), lambda bi, i: (bi, i, 0))
    ch_major = pl.BlockSpec((1, D_MODEL, tm), lambda bi, i: (bi, 0, i))
    flat_ch_major = pl.BlockSpec((D_MODEL, tm), lambda bi, i: (0, bi * nt + i))
    per_batch = pl.BlockSpec((1, 1, D_MODEL), lambda bi, i: (bi, 0, 0))
    rope_tab = pl.BlockSpec((HALF, tm), lambda bi, i: (0, i))
    col64 = pl.BlockSpec((HEAD_DIM, 1), lambda bi, i: (0, 0))
    return pl.pallas_call(
        functools.partial(_inproj_kernel, q_mult=q_mult),
        grid=(b, nt),
        in_specs=[tok_major, per_batch, per_batch,
                  pl.BlockSpec((1, D_MODEL), lambda bi, i: (0, 0)),
                  pl.BlockSpec((N_SECTIONS * D_MODEL, D_MODEL), lambda bi, i: (0, 0)),
                  rope_tab, rope_tab, col64, col64],
        out_specs=[ch_major, tok_major, ch_major, ch_major, flat_ch_major, flat_ch_major],
        out_shape=[jax.ShapeDtypeStruct((b, D_MODEL, s), BF16),
                   jax.ShapeDtypeStruct((b, s, D_MODEL), BF16),
                   jax.ShapeDtypeStruct((b, D_MODEL, s), BF16),
                   jax.ShapeDtypeStruct((b, D_MODEL, s), BF16),
                   jax.ShapeDtypeStruct((D_MODEL, b * s), BF16),
                   jax.ShapeDtypeStruct((D_MODEL, b * s), BF16)],
        compiler_params=pltpu.CompilerParams(
            dimension_semantics=("parallel", "parallel"), vmem_limit_bytes=VMEM_LIMIT),
        name="in_proj",
    )(x, shift, scale, norm_g, w_in_t, cos_t, sin_t, q_g, k_g)


def _attn_kernel(lam_ref, hg_ref, q_ref, k_ref, v_ref, ga_ref, o_ref,
                 acc_scr, m_scr, l_scr, s_scr, c_scr, *, blk, heads, lam_init):
    qi = pl.program_id(2)
    n_streams = 2 * heads
    row = lax.broadcasted_iota(jnp.int32, (V_HEAD, blk), 0)
    q_maps = []
    for h in range(heads):
        qt = q_ref[0, h * V_HEAD:(h + 1) * V_HEAD, :]
        zero = jnp.zeros_like(qt)
        q_maps += [jnp.where(row < HEAD_DIM, qt, zero), jnp.where(row >= HEAD_DIM, qt, zero)]
    acc_scr[...] = jnp.zeros_like(acc_scr)
    m_scr[...] = jnp.full_like(m_scr, MASK_VALUE)
    l_scr[...] = jnp.zeros_like(l_scr)

    kv_chunk = lax.broadcasted_iota(jnp.int32, (blk, blk), 0) >> CAUSAL_CHUNK_SHIFT
    q_chunk = lax.broadcasted_iota(jnp.int32, (blk, blk), 1) >> CAUSAL_CHUNK_SHIFT

    def scores(j, allowed):
        rows = pl.ds(pl.multiple_of(j * blk, blk), blk)
        for n in range(n_streams):
            h = n // 2
            kt = k_ref[0, rows, h * V_HEAD:(h + 1) * V_HEAD]
            s = jnp.dot(kt, q_maps[n], preferred_element_type=F32)
            if allowed is not None:
                s = jnp.where(allowed, s, MASK_VALUE)
            s_scr[n] = s
            c_scr[n] = jnp.max(s, axis=0, keepdims=True)

    def probs():
        out = []
        for n in range(n_streams):
            m_old = m_scr[n]
            m_new = jnp.maximum(m_old, c_scr[n])
            alpha = jnp.exp2(m_old - m_new)
            p = jnp.exp2(s_scr[n] - m_new)
            l_scr[n] = alpha * l_scr[n] + jnp.sum(p, axis=0, keepdims=True)
            m_scr[n] = m_new
            out.append((alpha, p.astype(BF16)))
        return out

    def accumulate(j, alpha_p):
        cols = pl.ds(pl.multiple_of(j * blk, blk), blk)
        for n, (alpha, p) in enumerate(alpha_p):
            h = n // 2
            vt = v_ref[0, h * V_HEAD:(h + 1) * V_HEAD, cols]
            acc_scr[n] = alpha * acc_scr[n] + jnp.dot(vt, p, preferred_element_type=F32)

    def step(j, next_allowed):
        alpha_p = probs()
        scores(j + 1, next_allowed)
        accumulate(j, alpha_p)

    scores(0, kv_chunk <= q_chunk + qi * (blk >> CAUSAL_CHUNK_SHIFT))

    def body(j, carry):
        step(j, None)
        return carry

    lax.fori_loop(0, qi - 1, body, 0)

    @pl.when(qi >= 1)
    def _():
        step(qi - 1, kv_chunk <= q_chunk)

    accumulate(qi, probs())

    lam_v = lam_ref[...]
    e1 = jnp.exp(jnp.sum(lam_v[0:1] * lam_v[1:2], axis=1, keepdims=True))
    e2 = jnp.exp(jnp.sum(lam_v[2:3] * lam_v[3:4], axis=1, keepdims=True))
    lam = e1 - e2 + lam_init
    for h in range(heads):
        o = (acc_scr[2 * h] / l_scr[2 * h]
             - lam * (acc_scr[2 * h + 1] / l_scr[2 * h + 1]))
        ms = jnp.mean(o * o, axis=0, keepdims=True)
        o = o * lax.rsqrt(ms + NORM_EPS) * hg_ref[...] * (1.0 - lam_init)
        head_rows = slice(h * V_HEAD, (h + 1) * V_HEAD)
        o_ref[0, head_rows, :] = (o * ga_ref[0, head_rows, :].astype(F32)).astype(BF16)


def _diff_attention(lam_vecs, head_g, q_t, k, v_t, ga_t, *, blk, heads, lam_init):
    b, _, s = q_t.shape
    nq = s // blk
    n_streams = 2 * heads
    width = heads * V_HEAD
    head_tile = pl.BlockSpec((1, width, blk), lambda bi, h, i: (bi, h, i))
    return pl.pallas_call(
        functools.partial(_attn_kernel, blk=blk, heads=heads, lam_init=lam_init),
        grid=(b, N_HEADS // heads, nq),
        in_specs=[pl.BlockSpec((4, HEAD_DIM), lambda bi, h, i: (0, 0)),
                  pl.BlockSpec((V_HEAD, 1), lambda bi, h, i: (0, 0)),
                  head_tile,
                  pl.BlockSpec((1, s, width), lambda bi, h, i: (bi, 0, h)),
                  pl.BlockSpec((1, width, s), lambda bi, h, i: (bi, h, 0)),
                  head_tile],
        out_specs=head_tile,
        out_shape=jax.ShapeDtypeStruct((b, D_MODEL, s), BF16),
        scratch_shapes=[pltpu.VMEM((n_streams, V_HEAD, blk), F32),
                        pltpu.VMEM((n_streams, 1, blk), F32),
                        pltpu.VMEM((n_streams, 1, blk), F32),
                        pltpu.VMEM((n_streams, blk, blk), F32),
                        pltpu.VMEM((n_streams, 1, blk), F32)],
        compiler_params=pltpu.CompilerParams(
            dimension_semantics=("parallel", "parallel", "arbitrary"), vmem_limit_bytes=VMEM_LIMIT),
        name="diff_attn",
    )(lam_vecs, head_g, q_t, k, v_t, ga_t)


def _zoh(a_re, a_im, log_dt):
    dt = jnp.exp(log_dt)
    mag = jnp.exp(a_re * dt)
    lre = mag * jnp.cos(a_im * dt)
    lim = mag * jnp.sin(a_im * dt)
    nr = lre - 1.0
    den = a_re * a_re + a_im * a_im
    fre = (nr * a_re + lim * a_im) / den
    fim = (lim * a_re - nr * a_im) / den
    return lre, lim, fre, fim


def _cpow(bre, bim, e, nbits):
    bre = jnp.broadcast_to(bre, e.shape)
    bim = jnp.broadcast_to(bim, e.shape)
    rre = jnp.ones(e.shape, F32)
    rim = jnp.zeros(e.shape, F32)
    for bit in range(nbits):
        take = ((e >> bit) & 1) == 1
        rre, rim = (jnp.where(take, rre * bre - rim * bim, rre),
                    jnp.where(take, rre * bim + rim * bre, rim))
        bre, bim = bre * bre - bim * bim, 2.0 * bre * bim
    return rre, rim


def _ssm_kernel(u_ref, prow_ref, pcol_ref, bre_ref, bim_ref, x1_ref, x2_ref, cre_ref, cim_ref,
                uu_ref, vv_ref, d_ref, y_ref, kt_ref, pq_ref, t_ref, b_ref, yacc_ref, st_ref, *, n_chunks):
    nbc = u_ref.shape[2]
    n_log = SSM_CHUNK.bit_length() - 1
    prow = prow_ref[0]
    lre, lim, fre, fim = _zoh(prow[0:1], prow[1:2], prow[2:3])
    pcol = pcol_ref[0]
    lre_c, lim_c, _, _ = _zoh(pcol[:, 0:1], pcol[:, 1:2], pcol[:, 2:3])

    lane = lax.broadcasted_iota(jnp.int32, (LANES, LANES), 1)
    sub = lax.broadcasted_iota(jnp.int32, (LANES, LANES), 0)
    ltre, ltim = _cpow(lre_c, lim_c, lane, n_log)
    lt1re = ltre * lre_c - ltim * lim_c
    lt1im = ltre * lim_c + ltim * lre_c
    lsre, lsim = _cpow(lre, lim, (SSM_CHUNK - 1) - sub, n_log)

    fre_h = fre[:, :SSM_STATE]
    fim_h = fim[:, :SSM_STATE]
    bbre = fre_h * bre_ref[0] - fim_h * bim_ref[0]
    bbim = fre_h * bim_ref[0] + fim_h * bre_ref[0]
    pair = (SSM_GROUP, SSM_GROUP, SSM_STATE)
    flat = (SSM_GROUP * SSM_GROUP, SSM_STATE)
    c_re = jnp.broadcast_to(cre_ref[0][:, None, :], pair).reshape(flat)
    c_im = jnp.broadcast_to(cim_ref[0][:, None, :], pair).reshape(flat)
    b_re = jnp.broadcast_to(bbre[None], pair).reshape(flat)
    b_im = jnp.broadcast_to(bbim[None], pair).reshape(flat)
    mre = c_re * b_re - c_im * b_im
    mim = c_re * b_im + c_im * b_re
    kt_ref[...] = (jnp.dot(mre, ltre[:SSM_STATE], preferred_element_type=F32, precision=lax.Precision.HIGHEST)
                   - jnp.dot(mim, ltim[:SSM_STATE], preferred_element_type=F32, precision=lax.Precision.HIGHEST))

    pq_ref[0] = fre * x1_ref[0] + fim * x2_ref[0]
    pq_ref[1] = fre * x2_ref[0] - fim * x1_ref[0]

    causal = lane >= sub

    n_pairs = SSM_GROUP // 2

    def build(i):
        t_rows, b_rows = [], []
        for dh in range(2):
            hi = 2 * i + dh
            blocks = []
            for ho in range(SSM_GROUP):
                lag = jnp.broadcast_to(kt_ref[pl.ds(ho * SSM_GROUP + hi, 1), :], (LANES, LANES))
                toep = pltpu.roll(lag, 0, 1, stride=1, stride_axis=0)
                blocks.append(jnp.where(causal, toep, 0.0).astype(BF16))
            t_rows.append(jnp.concatenate(blocks, axis=1))
            b_rows.append((lsre * pq_ref[0, pl.ds(hi, 1), :] + lsim * pq_ref[1, pl.ds(hi, 1), :]).astype(BF16))
        t_ref[...] = jnp.concatenate(t_rows, axis=0)
        b_ref[...] = jnp.concatenate(b_rows, axis=0)

    build(0)
    yacc_ref[...] = jnp.zeros_like(yacc_ref)
    st_ref[...] = jnp.zeros_like(st_ref)

    def pair_body(i, carry):
        x = jnp.concatenate([u_ref[0, 2 * i], u_ref[0, 2 * i + 1]], axis=1)
        yacc_ref[...] += jnp.dot(x, t_ref[...], preferred_element_type=F32)
        st_ref[...] += jnp.dot(x, b_ref[...], preferred_element_type=F32)
        build(jnp.minimum(i + 1, n_pairs - 1))
        return carry

    lax.fori_loop(0, n_pairs, pair_body, 0)
    y_acc = yacc_ref[...]
    st_acc = st_ref[...]

    are, aim = lre, lim
    for _ in range(n_log):
        are, aim = are * are - aim * aim, 2.0 * are * aim
    half_sign = jnp.where(lax.broadcasted_iota(jnp.int32, (1, LANES), 1) < SSM_STATE, -1.0, 1.0)
    chunk_id = lax.broadcasted_iota(jnp.int32, (nbc, LANES), 0) % n_chunks
    acc = st_acc
    dist = 1
    while dist < n_chunks:
        prev = jnp.where(chunk_id >= dist, pltpu.roll(acc, dist, 0), 0.0)
        acc = acc + are * prev + (aim * half_sign) * pltpu.roll(prev, SSM_STATE, 1)
        are, aim = are * are - aim * aim, 2.0 * are * aim
        dist *= 2
    h_in = jnp.where(chunk_id >= 1, pltpu.roll(acc, 1, 0), 0.0).astype(BF16)

    cm = [(lt1re * uu_ref[0][:, ho:ho + 1] + lt1im * vv_ref[0][:, ho:ho + 1]).astype(BF16)
          for ho in range(SSM_GROUP)]
    y_all = y_acc + jnp.dot(h_in, jnp.concatenate(cm, axis=1), preferred_element_type=F32)

    for h in range(SSM_GROUP):
        y = y_all[:, h * LANES:(h + 1) * LANES] + d_ref[0, h] * u_ref[0, h].astype(F32)
        y_ref[0, h] = y.astype(BF16)


def _s5_groups(u_t, a_re, a_im, log_dt, b_re, b_im, c_re, c_im, d_skip, *, n_chunks):
    nbc = u_t.shape[1] // SSM_CHUNK
    u4 = u_t.reshape(SSM_GROUPS, SSM_GROUP, nbc, SSM_CHUNK)
    g = SSM_GROUPS
    dup = lambda v: jnp.concatenate([v, v], axis=-1)
    ldt = jnp.broadcast_to(log_dt[:, None], (g, LANES))
    prow = jnp.concatenate([jnp.stack([dup(a_re), dup(a_im), ldt], axis=1),
                            jnp.zeros((g, 5, LANES), F32)], axis=1)
    pcol = prow.transpose(0, 2, 1)
    bre_t = b_re.transpose(0, 2, 1)
    bim_t = b_im.transpose(0, 2, 1)
    x1 = jnp.concatenate([bre_t, bim_t], axis=-1)
    x2 = jnp.concatenate([-bim_t, bre_t], axis=-1)
    cre_t = c_re.transpose(0, 2, 1)
    cim_t = c_im.transpose(0, 2, 1)
    uu = jnp.concatenate([cre_t, -cim_t], axis=1)
    vv = jnp.concatenate([-cim_t, -cre_t], axis=1)
    d_b = jnp.broadcast_to(d_skip.reshape(g, SSM_GROUP, 1, 1), (g, SSM_GROUP, 1, LANES))

    def per_group(*tail):
        return pl.BlockSpec((1,) + tail, lambda gi: (gi,) + (0,) * len(tail))

    y4 = pl.pallas_call(
        functools.partial(_ssm_kernel, n_chunks=n_chunks),
        grid=(g,),
        in_specs=[per_group(SSM_GROUP, nbc, SSM_CHUNK), per_group(8, LANES), per_group(LANES, 8),
                  per_group(SSM_GROUP, SSM_STATE), per_group(SSM_GROUP, SSM_STATE),
                  per_group(SSM_GROUP, LANES), per_group(SSM_GROUP, LANES),
                  per_group(SSM_GROUP, SSM_STATE), per_group(SSM_GROUP, SSM_STATE),
                  per_group(LANES, SSM_GROUP), per_group(LANES, SSM_GROUP),
                  per_group(SSM_GROUP, 1, LANES)],
        out_specs=per_group(SSM_GROUP, nbc, SSM_CHUNK),
        out_shape=jax.ShapeDtypeStruct(u4.shape, BF16),
        scratch_shapes=[pltpu.VMEM((SSM_GROUP * SSM_GROUP, LANES), F32),
                        pltpu.VMEM((2, SSM_GROUP, LANES), F32),
                        pltpu.VMEM((2 * SSM_CHUNK, SSM_GROUP * SSM_CHUNK), BF16),
                        pltpu.VMEM((2 * SSM_CHUNK, LANES), BF16),
                        pltpu.VMEM((nbc, SSM_GROUP * SSM_CHUNK), F32),
                        pltpu.VMEM((nbc, LANES), F32)],
        compiler_params=pltpu.CompilerParams(
            dimension_semantics=("parallel",), vmem_limit_bytes=VMEM_LIMIT),
        name="s5_groups",
    )(u4, prow, pcol, bre_t, bim_t, x1, x2, c_re, c_im, uu, vv, d_b)
    return y4.reshape(u_t.shape)


def _out_kernel(o_ref, y_ref, gs_ref, x_ref, gate_ref, wg_ref, bg_ref, wo_ref, out_ref):
    z0 = jax.nn.gelu(y_ref[...].astype(F32))
    t = jnp.dot(wg_ref[...], z0.astype(BF16), preferred_element_type=F32) + bg_ref[...]
    z = z0 * jax.nn.sigmoid(t) * gs_ref[...].astype(F32)
    mixed_t = (jnp.dot(wo_ref[:, :D_MODEL], o_ref[0], preferred_element_type=F32)
               + jnp.dot(wo_ref[:, D_MODEL:], z.astype(BF16), preferred_element_type=F32))
    out_ref[0] = x_ref[0] + gate_ref[0] * mixed_t.T


def _mix_out(o_t, y_t, gs_t, x, gate, w_glu_t, b_glu, w_out_t, *, tn):
    b, s, _ = x.shape
    nt = s // tn
    flat_ch_major = pl.BlockSpec((D_MODEL, tn), lambda bi, i: (0, bi * nt + i))
    tok_major = pl.BlockSpec((1, tn, D_MODEL), lambda bi, i: (bi, i, 0))
    return pl.pallas_call(
        _out_kernel,
        grid=(b, nt),
        in_specs=[pl.BlockSpec((1, D_MODEL, tn), lambda bi, i: (bi, 0, i)),
                  flat_ch_major, flat_ch_major, tok_major,
                  pl.BlockSpec((1, 1, D_MODEL), lambda bi, i: (bi, 0, 0)),
                  pl.BlockSpec((D_MODEL, D_MODEL), lambda bi, i: (0, 0)),
                  pl.BlockSpec((D_MODEL, 1), lambda bi, i: (0, 0)),
                  pl.BlockSpec((D_MODEL, 2 * D_MODEL), lambda bi, i: (0, 0))],
        out_specs=tok_major,
        out_shape=jax.ShapeDtypeStruct(x.shape, x.dtype),
        compiler_params=pltpu.CompilerParams(
            dimension_semantics=("parallel", "parallel"), vmem_limit_bytes=VMEM_LIMIT),
        name="mix_out",
    )(o_t, y_t, gs_t, x, gate, w_glu_t, b_glu, w_out_t)


def kernel(x, c, w_ada, b_ada, norm_g, w_in, q_norm_g, k_norm_g, lam_q1, lam_k1, lam_q2, lam_k2,
           head_norm_g, ssm_a_re, ssm_a_im, ssm_log_dt, ssm_b_re, ssm_b_im, ssm_c_re, ssm_c_im,
           ssm_d, w_glu, b_glu, w_out):
    b, s, _ = x.shape
    depth = w_in.shape[0]
    tok_tile = min(512, s)
    attn_tile = min(512, s)
    assert s % tok_tile == 0 and s % attn_tile == 0 and s % SSM_CHUNK == 0 and tok_tile % LANES == 0

    pos = jnp.arange(s, dtype=F32)
    inv_freq = 1.0 / (ROPE_THETA ** (jnp.arange(0, HEAD_DIM, 2, dtype=F32) / HEAD_DIM))
    ang_t = inv_freq[:, None] * pos[None, :]
    cos_t, sin_t = jnp.cos(ang_t), jnp.sin(ang_t)

    for l in range(depth):
        lam_init = 0.8 - 0.6 * math.exp(-0.3 * l)
        mod = _ada_mod(c, w_ada[l], b_ada[l])
        shift, scale, gate = (mod[:, i * D_MODEL:(i + 1) * D_MODEL].reshape(b, 1, D_MODEL) for i in range(3))
        q_t, k, v_t, ga_t, u_t, gs_t = _in_proj(
            x, shift, scale, norm_g[l].reshape(1, D_MODEL), w_in[l].T.astype(BF16), cos_t, sin_t,
            q_norm_g[l].reshape(HEAD_DIM, 1), k_norm_g[l].reshape(HEAD_DIM, 1), tm=tok_tile)
        lam_vecs = jnp.stack([lam_q1[l], lam_k1[l], lam_q2[l], lam_k2[l]]).astype(F32)
        o_t = _diff_attention(lam_vecs, head_norm_g[l].reshape(V_HEAD, 1), q_t, k, v_t, ga_t,
                              blk=attn_tile, heads=ATTN_HEADS_PER_STEP, lam_init=lam_init)
        y_t = _s5_groups(u_t, ssm_a_re[l], ssm_a_im[l], ssm_log_dt[l], ssm_b_re[l], ssm_b_im[l],
                         ssm_c_re[l], ssm_c_im[l], ssm_d[l], n_chunks=s // SSM_CHUNK)
        x = _mix_out(o_t, y_t, gs_t, x, gate, w_glu[l].T.astype(BF16), b_glu[l].reshape(D_MODEL, 1),
                     w_out[l].T.astype(BF16), tn=tok_tile)
    return x
```

```python
import functools
import math

import jax
import jax.numpy as jnp
from jax import lax
from jax.experimental import pallas as pl
from jax.experimental.pallas import tpu as pltpu

F32 = jnp.float32
BF16 = jnp.bfloat16

D_MODEL = 1024
HEAD_DIM = 64
HALF = HEAD_DIM // 2
V_HEAD = 2 * HEAD_DIM
N_HEADS = D_MODEL // V_HEAD
SSM_GROUPS = 64
SSM_GROUP = 16
SSM_STATE = 64
N_SECTIONS = 6
ROPE_THETA = 10000.0
NORM_EPS = 1e-6
MASK_VALUE = -1e30
CAUSAL_CHUNK_SHIFT = 6
LANES = 128
SSM_CHUNK = LANES
VMEM_LIMIT = 56 * 1024 * 1024
ATTN_HEADS_PER_STEP = 4
SSM_GROUPS_PER_STEP = 2


def _silu(v):
    return v * jax.nn.sigmoid(v)


def _ada_kernel(c_ref, w_ref, b_ref, o_ref):
    c = c_ref[...]
    o_ref[...] = jnp.dot(_silu(c), w_ref[...], preferred_element_type=F32) + b_ref[...]


def _ada_mod(c, w_ada, b_ada):
    b = c.shape[0]
    return pl.pallas_call(
        _ada_kernel,
        grid=(3,),
        in_specs=[pl.BlockSpec((b, D_MODEL), lambda j: (0, 0)),
                  pl.BlockSpec((D_MODEL, D_MODEL), lambda j: (0, j)),
                  pl.BlockSpec((1, D_MODEL), lambda j: (0, j))],
        out_specs=pl.BlockSpec((b, D_MODEL), lambda j: (0, j)),
        out_shape=jax.ShapeDtypeStruct((b, 3 * D_MODEL), F32),
        name="ada_mod",
    )(c, w_ada, b_ada.reshape(1, 3 * D_MODEL))


def _inproj_kernel(x_ref, shift_ref, scale_ref, ng_ref, wt_ref, cos_ref, sin_ref, qg_ref, kg_ref,
                   q_ref, k_ref, v_ref, ga_ref, u_ref, gs_ref, *, q_mult):
    tm = x_ref.shape[1]
    x = x_ref[0]
    ms = jnp.mean(x * x, axis=-1, keepdims=True)
    y = x * lax.rsqrt(ms + NORM_EPS) * ng_ref[...]
    h = (y * (1.0 + scale_ref[0]) + shift_ref[0]).astype(BF16)
    cos = cos_ref[...][None]
    sin = sin_ref[...][None]

    def proj(sec):
        w = wt_ref[sec * D_MODEL:(sec + 1) * D_MODEL, :]
        return lax.dot_general(w, h, (((1,), (1,)), ((), ())), preferred_element_type=F32)

    def norm_rope(r, g_ref):
        r3 = r.reshape(2 * N_HEADS, HEAD_DIM, tm)
        ss = jnp.mean(r3 * r3, axis=1, keepdims=True)
        yn = r3 * lax.rsqrt(ss + NORM_EPS) * g_ref[...][None]
        x1 = yn[:, :HALF, :]
        x2 = yn[:, HALF:, :]
        o = jnp.concatenate([x1 * cos - x2 * sin, x2 * cos + x1 * sin], axis=1)
        return o.reshape(D_MODEL, tm)

    q_ref[0] = (norm_rope(proj(0), qg_ref) * q_mult).astype(BF16)
    k_ref[0] = norm_rope(proj(1), kg_ref).T.astype(BF16)
    v_ref[0] = proj(2).astype(BF16)
    ga_ref[0] = _silu(proj(3)).astype(BF16)
    u_ref[...] = proj(4).astype(BF16)
    gs_ref[...] = _silu(proj(5)).astype(BF16)


def _in_proj(x, shift, scale, norm_g, w_in_t, cos_t, sin_t, q_g, k_g, *, tm):
    b, s, _ = x.shape
    nt = s // tm
    q_mult = HEAD_DIM ** -0.5 * math.log2(math.e)
    tok_major = pl.BlockSpec((1, tm, D_MODEL), lambda bi, i: (bi, i, 0))
    ch_major = pl.BlockSpec((1, D_MODEL, tm), lambda bi, i: (bi, 0, i))
    flat_ch_major = pl.BlockSpec((D_MODEL, tm), lambda bi, i: (0, bi * nt + i))
    per_batch = pl.BlockSpec((1, 1, D_MODEL), lambda bi, i: (bi, 0, 0))
    rope_tab = pl.BlockSpec((HALF, tm), lambda bi, i: (0, i))
    col64 = pl.BlockSpec((HEAD_DIM, 1), lambda bi, i: (0, 0))
    return pl.pallas_call(
        functools.partial(_inproj_kernel, q_mult=q_mult),
        grid=(b, nt),
        in_specs=[tok_major, per_batch, per_batch,
                  pl.BlockSpec((1, D_MODEL), lambda bi, i: (0, 0)),
                  pl.BlockSpec((N_SECTIONS * D_MODEL, D_MODEL), lambda bi, i: (0, 0)),
                  rope_tab, rope_tab, col64, col64],
        out_specs=[ch_major, tok_major, ch_major, ch_major, flat_ch_major, flat_ch_major],
        out_shape=[jax.ShapeDtypeStruct((b, D_MODEL, s), BF16),
                   jax.ShapeDtypeStruct((b, s, D_MODEL), BF16),
                   jax.ShapeDtypeStruct((b, D_MODEL, s), BF16),
                   jax.ShapeDtypeStruct((b, D_MODEL, s), BF16),
                   jax.ShapeDtypeStruct((D_MODEL, b * s), BF16),
                   jax.ShapeDtypeStruct((D_MODEL, b * s), BF16)],
        compiler_params=pltpu.CompilerParams(
            dimension_semantics=("parallel", "parallel"), vmem_limit_bytes=VMEM_LIMIT),
        name="in_proj",
    )(x, shift, scale, norm_g, w_in_t, cos_t, sin_t, q_g, k_g)


def _attn_kernel(lam_ref, hg_ref, q_ref, k_ref, v_ref, ga_ref, o_ref,
                 acc_scr, m_scr, l_scr, s_scr, c_scr, *, blk, heads, lam_init):
    qi = pl.program_id(2)
    n_streams = 2 * heads
    row = lax.broadcasted_iota(jnp.int32, (V_HEAD, blk), 0)
    q_maps = []
    for h in range(heads):
        qt = q_ref[0, h * V_HEAD:(h + 1) * V_HEAD, :]
        zero = jnp.zeros_like(qt)
        q_maps += [jnp.where(row < HEAD_DIM, qt, zero), jnp.where(row >= HEAD_DIM, qt, zero)]
    acc_scr[...] = jnp.zeros_like(acc_scr)
    m_scr[...] = jnp.full_like(m_scr, MASK_VALUE)
    l_scr[...] = jnp.zeros_like(l_scr)

    kv_chunk = lax.broadcasted_iota(jnp.int32, (blk, blk), 0) >> CAUSAL_CHUNK_SHIFT
    q_chunk = lax.broadcasted_iota(jnp.int32, (blk, blk), 1) >> CAUSAL_CHUNK_SHIFT

    def scores(j, allowed):
        rows = pl.ds(pl.multiple_of(j * blk, blk), blk)
        for n in range(n_streams):
            h = n // 2
            kt = k_ref[0, rows, h * V_HEAD:(h + 1) * V_HEAD]
            s = jnp.dot(kt, q_maps[n], preferred_element_type=F32)
            if allowed is not None:
                s = jnp.where(allowed, s, MASK_VALUE)
            s_scr[n] = s
            c_scr[n] = jnp.max(s, axis=0, keepdims=True)

    def probs():
        out = []
        for n in range(n_streams):
            m_old = m_scr[n]
            m_new = jnp.maximum(m_old, c_scr[n])
            alpha = jnp.exp2(m_old - m_new)
            p = jnp.exp2(s_scr[n] - m_new)
            l_scr[n] = alpha * l_scr[n] + jnp.sum(p, axis=0, keepdims=True)
            m_scr[n] = m_new
            out.append((alpha, p.astype(BF16)))
        return out

    def accumulate(j, alpha_p):
        cols = pl.ds(pl.multiple_of(j * blk, blk), blk)
        for n, (alpha, p) in enumerate(alpha_p):
            h = n // 2
            vt = v_ref[0, h * V_HEAD:(h + 1) * V_HEAD, cols]
            acc_scr[n] = alpha * acc_scr[n] + jnp.dot(vt, p, preferred_element_type=F32)

    def step(j, next_allowed):
        alpha_p = probs()
        scores(j + 1, next_allowed)
        accumulate(j, alpha_p)

    scores(0, kv_chunk <= q_chunk + qi * (blk >> CAUSAL_CHUNK_SHIFT))

    def body(j, carry):
        step(j, None)
        return carry

    lax.fori_loop(0, qi - 1, body, 0)

    @pl.when(qi >= 1)
    def _():
        step(qi - 1, kv_chunk <= q_chunk)

    accumulate(qi, probs())

    lam_v = lam_ref[...]
    e1 = jnp.exp(jnp.sum(lam_v[0:1] * lam_v[1:2], axis=1, keepdims=True))
    e2 = jnp.exp(jnp.sum(lam_v[2:3] * lam_v[3:4], axis=1, keepdims=True))
    lam = e1 - e2 + lam_init
    for h in range(heads):
        o = (acc_scr[2 * h] / l_scr[2 * h]
             - lam * (acc_scr[2 * h + 1] / l_scr[2 * h + 1]))
        ms = jnp.mean(o * o, axis=0, keepdims=True)
        o = o * lax.rsqrt(ms + NORM_EPS) * hg_ref[...] * (1.0 - lam_init)
        head_rows = slice(h * V_HEAD, (h + 1) * V_HEAD)
        o_ref[0, head_rows, :] = (o * ga_ref[0, head_rows, :].astype(F32)).astype(BF16)


def _diff_attention(lam_vecs, head_g, q_t, k, v_t, ga_t, *, blk, heads, lam_init):
    b, _, s = q_t.shape
    nq = s // blk
    n_streams = 2 * heads
    width = heads * V_HEAD
    head_tile = pl.BlockSpec((1, width, blk), lambda bi, h, i: (bi, h, i))
    return pl.pallas_call(
        functools.partial(_attn_kernel, blk=blk, heads=heads, lam_init=lam_init),
        grid=(b, N_HEADS // heads, nq),
        in_specs=[pl.BlockSpec((4, HEAD_DIM), lambda bi, h, i: (0, 0)),
                  pl.BlockSpec((V_HEAD, 1), lambda bi, h, i: (0, 0)),
                  head_tile,
                  pl.BlockSpec((1, s, width), lambda bi, h, i: (bi, 0, h)),
                  pl.BlockSpec((1, width, s), lambda bi, h, i: (bi, h, 0)),
                  head_tile],
        out_specs=head_tile,
        out_shape=jax.ShapeDtypeStruct((b, D_MODEL, s), BF16),
        scratch_shapes=[pltpu.VMEM((n_streams, V_HEAD, blk), F32),
                        pltpu.VMEM((n_streams, 1, blk), F32),
                        pltpu.VMEM((n_streams, 1, blk), F32),
                        pltpu.VMEM((n_streams, blk, blk), F32),
                        pltpu.VMEM((n_streams, 1, blk), F32)],
        compiler_params=pltpu.CompilerParams(
            dimension_semantics=("parallel", "parallel", "arbitrary"), vmem_limit_bytes=VMEM_LIMIT),
        name="diff_attn",
    )(lam_vecs, head_g, q_t, k, v_t, ga_t)


def _zoh(a_re, a_im, log_dt):
    dt = jnp.exp(log_dt)
    mag = jnp.exp(a_re * dt)
    lre = mag * jnp.cos(a_im * dt)
    lim = mag * jnp.sin(a_im * dt)
    nr = lre - 1.0
    den = a_re * a_re + a_im * a_im
    fre = (nr * a_re + lim * a_im) / den
    fim = (lim * a_re - nr * a_im) / den
    return lre, lim, fre, fim


def _cpow(bre, bim, e, nbits):
    bre = jnp.broadcast_to(bre, e.shape)
    bim = jnp.broadcast_to(bim, e.shape)
    rre = jnp.ones(e.shape, F32)
    rim = jnp.zeros(e.shape, F32)
    for bit in range(nbits):
        take = ((e >> bit) & 1) == 1
        rre, rim = (jnp.where(take, rre * bre - rim * bim, rre),
                    jnp.where(take, rre * bim + rim * bre, rim))
        bre, bim = bre * bre - bim * bim, 2.0 * bre * bim
    return rre, rim


def _ssm_kernel(u_ref, prow_ref, pcol_ref, bre_ref, bim_ref, x1_ref, x2_ref, cre_ref, cim_ref,
                uu_ref, vv_ref, d_ref, y_ref, kt_ref, pq_ref, t_ref, b_ref, yacc_ref, st_ref, *, n_chunks, groups):
    nbc = u_ref.shape[2]
    n_log = SSM_CHUNK.bit_length() - 1
    n_pairs = SSM_GROUP // 2
    lane = lax.broadcasted_iota(jnp.int32, (LANES, LANES), 1)
    sub = lax.broadcasted_iota(jnp.int32, (LANES, LANES), 0)
    causal = lane >= sub

    def setup(g):
        prow = prow_ref[g]
        lre, lim, fre, fim = _zoh(prow[0:1], prow[1:2], prow[2:3])
        pcol = pcol_ref[g]
        lre_c, lim_c, _, _ = _zoh(pcol[:, 0:1], pcol[:, 1:2], pcol[:, 2:3])
        ltre, ltim = _cpow(lre_c, lim_c, lane, n_log)
        lt1re = ltre * lre_c - ltim * lim_c
        lt1im = ltre * lim_c + ltim * lre_c
        lsre, lsim = _cpow(lre, lim, (SSM_CHUNK - 1) - sub, n_log)

        fre_h = fre[:, :SSM_STATE]
        fim_h = fim[:, :SSM_STATE]
        bbre = fre_h * bre_ref[g] - fim_h * bim_ref[g]
        bbim = fre_h * bim_ref[g] + fim_h * bre_ref[g]
        pair = (SSM_GROUP, SSM_GROUP, SSM_STATE)
        flat = (SSM_GROUP * SSM_GROUP, SSM_STATE)
        c_re = jnp.broadcast_to(cre_ref[g][:, None, :], pair).reshape(flat)
        c_im = jnp.broadcast_to(cim_ref[g][:, None, :], pair).reshape(flat)
        b_re = jnp.broadcast_to(bbre[None], pair).reshape(flat)
        b_im = jnp.broadcast_to(bbim[None], pair).reshape(flat)
        mre = c_re * b_re - c_im * b_im
        mim = c_re * b_im + c_im * b_re
        kt_ref[g] = (jnp.dot(mre, ltre[:SSM_STATE], preferred_element_type=F32, precision=lax.Precision.HIGHEST)
                     - jnp.dot(mim, ltim[:SSM_STATE], preferred_element_type=F32, precision=lax.Precision.HIGHEST))
        pq_ref[g, 0] = fre * x1_ref[g] + fim * x2_ref[g]
        pq_ref[g, 1] = fre * x2_ref[g] - fim * x1_ref[g]
        return lre, lim, lsre, lsim, lt1re, lt1im

    tables = [setup(g) for g in range(groups)]

    def build(g, i):
        _, _, lsre, lsim, _, _ = tables[g]
        t_rows, b_rows = [], []
        for dh in range(2):
            hi = 2 * i + dh
            blocks = []
            for ho in range(SSM_GROUP):
                lag = jnp.broadcast_to(kt_ref[g, pl.ds(ho * SSM_GROUP + hi, 1), :], (LANES, LANES))
                toep = pltpu.roll(lag, 0, 1, stride=1, stride_axis=0)
                blocks.append(jnp.where(causal, toep, 0.0).astype(BF16))
            t_rows.append(jnp.concatenate(blocks, axis=1))
            b_rows.append((lsre * pq_ref[g, 0, pl.ds(hi, 1), :] + lsim * pq_ref[g, 1, pl.ds(hi, 1), :]).astype(BF16))
        t_ref[g] = jnp.concatenate(t_rows, axis=0)
        b_ref[g] = jnp.concatenate(b_rows, axis=0)

    for g in range(groups):
        build(g, 0)
    yacc_ref[...] = jnp.zeros_like(yacc_ref)
    st_ref[...] = jnp.zeros_like(st_ref)

    def pair_body(i, carry):
        for g in range(groups):
            x = jnp.concatenate([u_ref[g, 2 * i], u_ref[g, 2 * i + 1]], axis=1)
            yacc_ref[g] += jnp.dot(x, t_ref[g], preferred_element_type=F32)
            st_ref[g] += jnp.dot(x, b_ref[g], preferred_element_type=F32)
        for g in range(groups):
            build(g, jnp.minimum(i + 1, n_pairs - 1))
        return carry

    lax.fori_loop(0, n_pairs, pair_body, 0)

    half_sign = jnp.where(lax.broadcasted_iota(jnp.int32, (1, LANES), 1) < SSM_STATE, -1.0, 1.0)
    chunk_id = lax.broadcasted_iota(jnp.int32, (nbc, LANES), 0) % n_chunks
    for g in range(groups):
        lre, lim, _, _, lt1re, lt1im = tables[g]
        are, aim = lre, lim
        for _ in range(n_log):
            are, aim = are * are - aim * aim, 2.0 * are * aim
        acc = st_ref[g]
        dist = 1
        while dist < n_chunks:
            prev = jnp.where(chunk_id >= dist, pltpu.roll(acc, dist, 0), 0.0)
            acc = acc + are * prev + (aim * half_sign) * pltpu.roll(prev, SSM_STATE, 1)
            are, aim = are * are - aim * aim, 2.0 * are * aim
            dist *= 2
        h_in = jnp.where(chunk_id >= 1, pltpu.roll(acc, 1, 0), 0.0).astype(BF16)

        cm = [(lt1re * uu_ref[g][:, ho:ho + 1] + lt1im * vv_ref[g][:, ho:ho + 1]).astype(BF16)
              for ho in range(SSM_GROUP)]
        y_all = yacc_ref[g] + jnp.dot(h_in, jnp.concatenate(cm, axis=1), preferred_element_type=F32)

        for h in range(SSM_GROUP):
            y = y_all[:, h * LANES:(h + 1) * LANES] + d_ref[g, h] * u_ref[g, h].astype(F32)
            y_ref[g, h] = y.astype(BF16)


def _s5_groups(u_t, a_re, a_im, log_dt, b_re, b_im, c_re, c_im, d_skip, *, n_chunks):
    nbc = u_t.shape[1] // SSM_CHUNK
    u4 = u_t.reshape(SSM_GROUPS, SSM_GROUP, nbc, SSM_CHUNK)
    g = SSM_GROUPS
    dup = lambda v: jnp.concatenate([v, v], axis=-1)
    ldt = jnp.broadcast_to(log_dt[:, None], (g, LANES))
    prow = jnp.concatenate([jnp.stack([dup(a_re), dup(a_im), ldt], axis=1),
                            jnp.zeros((g, 5, LANES), F32)], axis=1)
    pcol = prow.transpose(0, 2, 1)
    bre_t = b_re.transpose(0, 2, 1)
    bim_t = b_im.transpose(0, 2, 1)
    x1 = jnp.concatenate([bre_t, bim_t], axis=-1)
    x2 = jnp.concatenate([-bim_t, bre_t], axis=-1)
    cre_t = c_re.transpose(0, 2, 1)
    cim_t = c_im.transpose(0, 2, 1)
    uu = jnp.concatenate([cre_t, -cim_t], axis=1)
    vv = jnp.concatenate([-cim_t, -cre_t], axis=1)
    d_b = jnp.broadcast_to(d_skip.reshape(g, SSM_GROUP, 1, 1), (g, SSM_GROUP, 1, LANES))

    gps = SSM_GROUPS_PER_STEP

    def per_group(*tail):
        return pl.BlockSpec((gps,) + tail, lambda gi: (gi,) + (0,) * len(tail))

    y4 = pl.pallas_call(
        functools.partial(_ssm_kernel, n_chunks=n_chunks, groups=gps),
        grid=(g // gps,),
        in_specs=[per_group(SSM_GROUP, nbc, SSM_CHUNK), per_group(8, LANES), per_group(LANES, 8),
                  per_group(SSM_GROUP, SSM_STATE), per_group(SSM_GROUP, SSM_STATE),
                  per_group(SSM_GROUP, LANES), per_group(SSM_GROUP, LANES),
                  per_group(SSM_GROUP, SSM_STATE), per_group(SSM_GROUP, SSM_STATE),
                  per_group(LANES, SSM_GROUP), per_group(LANES, SSM_GROUP),
                  per_group(SSM_GROUP, 1, LANES)],
        out_specs=per_group(SSM_GROUP, nbc, SSM_CHUNK),
        out_shape=jax.ShapeDtypeStruct(u4.shape, BF16),
        scratch_shapes=[pltpu.VMEM((gps, SSM_GROUP * SSM_GROUP, LANES), F32),
                        pltpu.VMEM((gps, 2, SSM_GROUP, LANES), F32),
                        pltpu.VMEM((gps, 2 * SSM_CHUNK, SSM_GROUP * SSM_CHUNK), BF16),
                        pltpu.VMEM((gps, 2 * SSM_CHUNK, LANES), BF16),
                        pltpu.VMEM((gps, nbc, SSM_GROUP * SSM_CHUNK), F32),
                        pltpu.VMEM((gps, nbc, LANES), F32)],
        compiler_params=pltpu.CompilerParams(
            dimension_semantics=("parallel",), vmem_limit_bytes=VMEM_LIMIT),
        name="s5_groups",
    )(u4, prow, pcol, bre_t, bim_t, x1, x2, c_re, c_im, uu, vv, d_b)
    return y4.reshape(u_t.shape)


def _out_kernel(o_ref, y_ref, gs_ref, x_ref, gate_ref, wg_ref, bg_ref, wo_ref, out_ref):
    z0 = jax.nn.gelu(y_ref[...].astype(F32))
    t = jnp.dot(wg_ref[...], z0.astype(BF16), preferred_element_type=F32) + bg_ref[...]
    z = z0 * jax.nn.sigmoid(t) * gs_ref[...].astype(F32)
    mixed_t = (jnp.dot(wo_ref[:, :D_MODEL], o_ref[0], preferred_element_type=F32)
               + jnp.dot(wo_ref[:, D_MODEL:], z.astype(BF16), preferred_element_type=F32))
    out_ref[0] = x_ref[0] + gate_ref[0] * mixed_t.T


def _mix_out(o_t, y_t, gs_t, x, gate, w_glu_t, b_glu, w_out_t, *, tn):
    b, s, _ = x.shape
    nt = s // tn
    flat_ch_major = pl.BlockSpec((D_MODEL, tn), lambda bi, i: (0, bi * nt + i))
    tok_major = pl.BlockSpec((1, tn, D_MODEL), lambda bi, i: (bi, i, 0))
    return pl.pallas_call(
        _out_kernel,
        grid=(b, nt),
        in_specs=[pl.BlockSpec((1, D_MODEL, tn), lambda bi, i: (bi, 0, i)),
                  flat_ch_major, flat_ch_major, tok_major,
                  pl.BlockSpec((1, 1, D_MODEL), lambda bi, i: (bi, 0, 0)),
                  pl.BlockSpec((D_MODEL, D_MODEL), lambda bi, i: (0, 0)),
                  pl.BlockSpec((D_MODEL, 1), lambda bi, i: (0, 0)),
                  pl.BlockSpec((D_MODEL, 2 * D_MODEL), lambda bi, i: (0, 0))],
        out_specs=tok_major,
        out_shape=jax.ShapeDtypeStruct(x.shape, x.dtype),
        compiler_params=pltpu.CompilerParams(
            dimension_semantics=("parallel", "parallel"), vmem_limit_bytes=VMEM_LIMIT),
        name="mix_out",
    )(o_t, y_t, gs_t, x, gate, w_glu_t, b_glu, w_out_t)


def kernel(x, c, w_ada, b_ada, norm_g, w_in, q_norm_g, k_norm_g, lam_q1, lam_k1, lam_q2, lam_k2,
           head_norm_g, ssm_a_re, ssm_a_im, ssm_log_dt, ssm_b_re, ssm_b_im, ssm_c_re, ssm_c_im,
           ssm_d, w_glu, b_glu, w_out):
    b, s, _ = x.shape
    depth = w_in.shape[0]
    tok_tile = min(512, s)
    attn_tile = min(512, s)
    assert s % tok_tile == 0 and s % attn_tile == 0 and s % SSM_CHUNK == 0 and tok_tile % LANES == 0

    pos = jnp.arange(s, dtype=F32)
    inv_freq = 1.0 / (ROPE_THETA ** (jnp.arange(0, HEAD_DIM, 2, dtype=F32) / HEAD_DIM))
    ang_t = inv_freq[:, None] * pos[None, :]
    cos_t, sin_t = jnp.cos(ang_t), jnp.sin(ang_t)

    for l in range(depth):
        lam_init = 0.8 - 0.6 * math.exp(-0.3 * l)
        mod = _ada_mod(c, w_ada[l], b_ada[l])
        shift, scale, gate = (mod[:, i * D_MODEL:(i + 1) * D_MODEL].reshape(b, 1, D_MODEL) for i in range(3))
        q_t, k, v_t, ga_t, u_t, gs_t = _in_proj(
            x, shift, scale, norm_g[l].reshape(1, D_MODEL), w_in[l].T.astype(BF16), cos_t, sin_t,
            q_norm_g[l].reshape(HEAD_DIM, 1), k_norm_g[l].reshape(HEAD_DIM, 1), tm=tok_tile)
        lam_vecs = jnp.stack([lam_q1[l], lam_k1[l], lam_q2[l], lam_k2[l]]).astype(F32)
        o_t = _diff_attention(lam_vecs, head_norm_g[l].reshape(V_HEAD, 1), q_t, k, v_t, ga_t,
                              blk=attn_tile, heads=ATTN_HEADS_PER_STEP, lam_init=lam_init)
        y_t = _s5_groups(u_t, ssm_a_re[l], ssm_a_im[l], ssm_log_dt[l], ssm_b_re[l], ssm_b_im[l],
                         ssm_c_re[l], ssm_c_im[l], ssm_d[l], n_chunks=s // SSM_CHUNK)
        x = _mix_out(o_t, y_t, gs_t, x, gate, w_glu[l].T.astype(BF16), b_glu[l].reshape(D_MODEL, 1),
                     w_out[l].T.astype(BF16), tn=tok_tile)
    return x
```

```python
import functools
import math

import jax
import jax.numpy as jnp
from jax import lax
from jax.experimental import pallas as pl
from jax.experimental.pallas import tpu as pltpu

F32 = jnp.float32
BF16 = jnp.bfloat16

D_MODEL = 1024
HEAD_DIM = 64
HALF = HEAD_DIM // 2
V_HEAD = 2 * HEAD_DIM
N_HEADS = D_MODEL // V_HEAD
SSM_GROUPS = 64
SSM_GROUP = 16
SSM_STATE = 64
N_SECTIONS = 6
ROPE_THETA = 10000.0
NORM_EPS = 1e-6
MASK_VALUE = -1e30
CAUSAL_CHUNK_SHIFT = 6
LANES = 128
SUBLANES = 8
SSM_CHUNK = LANES
VMEM_LIMIT = 56 * 1024 * 1024
ATTN_HEADS_PER_STEP = 4
SSM_GROUPS_PER_STEP = 2


def _silu(v):
    return v * jax.nn.sigmoid(v)


def _ada_kernel(c_ref, w_ref, b_ref, o_ref):
    c = c_ref[...]
    o_ref[...] = jnp.dot(_silu(c), w_ref[...], preferred_element_type=F32) + b_ref[...]


def _ada_mod(c, w_ada, b_ada):
    b = c.shape[0]
    return pl.pallas_call(
        _ada_kernel,
        grid=(3,),
        in_specs=[pl.BlockSpec((b, D_MODEL), lambda j: (0, 0)),
                  pl.BlockSpec((D_MODEL, D_MODEL), lambda j: (0, j)),
                  pl.BlockSpec((1, D_MODEL), lambda j: (0, j))],
        out_specs=pl.BlockSpec((b, D_MODEL), lambda j: (0, j)),
        out_shape=jax.ShapeDtypeStruct((b, 3 * D_MODEL), F32),
        name="ada_mod",
    )(c, w_ada, b_ada.reshape(1, 3 * D_MODEL))


def _inproj_kernel(x_ref, shift_ref, scale_ref, ng_ref, wt_ref, cos_ref, sin_ref, qg_ref, kg_ref,
                   q_ref, k_ref, v_ref, ga_ref, u_ref, gs_ref, *, q_mult):
    tm = x_ref.shape[1]
    x = x_ref[0]
    ms = jnp.mean(x * x, axis=-1, keepdims=True)
    y = x * lax.rsqrt(ms + NORM_EPS) * ng_ref[...]
    h = (y * (1.0 + scale_ref[0]) + shift_ref[0]).astype(BF16)
    cos = cos_ref[...][None]
    sin = sin_ref[...][None]

    def proj(sec):
        w = wt_ref[sec * D_MODEL:(sec + 1) * D_MODEL, :]
        return lax.dot_general(w, h, (((1,), (1,)), ((), ())), preferred_element_type=F32)

    def norm_rope(r, g_ref):
        r3 = r.reshape(2 * N_HEADS, HEAD_DIM, tm)
        ss = jnp.mean(r3 * r3, axis=1, keepdims=True)
        yn = r3 * lax.rsqrt(ss + NORM_EPS) * g_ref[...][None]
        x1 = yn[:, :HALF, :]
        x2 = yn[:, HALF:, :]
        o = jnp.concatenate([x1 * cos - x2 * sin, x2 * cos + x1 * sin], axis=1)
        return o.reshape(D_MODEL, tm)

    q_ref[0] = (norm_rope(proj(0), qg_ref) * q_mult).astype(BF16)
    k_ref[0] = norm_rope(proj(1), kg_ref).T.astype(BF16)
    v_ref[0] = proj(2).astype(BF16)
    ga_ref[0] = _silu(proj(3)).astype(BF16)
    u_ref[...] = proj(4).astype(BF16)
    gs_ref[...] = _silu(proj(5)).astype(BF16)


def _in_proj(x, shift, scale, norm_g, w_in_t, cos_t, sin_t, q_g, k_g, *, tm):
    b, s, _ = x.shape
    nt = s // tm
    q_mult = HEAD_DIM ** -0.5 * math.log2(math.e)
    tok_major = pl.BlockSpec((1, tm, D_MODEL), lambda bi, i: (bi, i, 0))
    ch_major = pl.BlockSpec((1, D_MODEL, tm), lambda bi, i: (bi, 0, i))
    flat_ch_major = pl.BlockSpec((D_MODEL, tm), lambda bi, i: (0, bi * nt + i))
    per_batch = pl.BlockSpec((1, 1, D_MODEL), lambda bi, i: (bi, 0, 0))
    rope_tab = pl.BlockSpec((HALF, tm), lambda bi, i: (0, i))
    col64 = pl.BlockSpec((HEAD_DIM, 1), lambda bi, i: (0, 0))
    return pl.pallas_call(
        functools.partial(_inproj_kernel, q_mult=q_mult),
        grid=(b, nt),
        in_specs=[tok_major, per_batch, per_batch,
                  pl.BlockSpec((1, D_MODEL), lambda bi, i: (0, 0)),
                  pl.BlockSpec((N_SECTIONS * D_MODEL, D_MODEL), lambda bi, i: (0, 0)),
                  rope_tab, rope_tab, col64, col64],
        out_specs=[ch_major, tok_major, ch_major, ch_major, flat_ch_major, flat_ch_major],
        out_shape=[jax.ShapeDtypeStruct((b, D_MODEL, s), BF16),
                   jax.ShapeDtypeStruct((b, s, D_MODEL), BF16),
                   jax.ShapeDtypeStruct((b, D_MODEL, s), BF16),
                   jax.ShapeDtypeStruct((b, D_MODEL, s), BF16),
                   jax.ShapeDtypeStruct((D_MODEL, b * s), BF16),
                   jax.ShapeDtypeStruct((D_MODEL, b * s), BF16)],
        compiler_params=pltpu.CompilerParams(
            dimension_semantics=("parallel", "parallel"), vmem_limit_bytes=VMEM_LIMIT),
        name="in_proj",
    )(x, shift, scale, norm_g, w_in_t, cos_t, sin_t, q_g, k_g)


def _attn_kernel(lam_ref, hg_ref, q_ref, k_ref, v_ref, ga_ref, o_ref,
                 acc_scr, m_scr, l_scr, s_scr, c_scr, *, blk, heads, lam_init):
    qi = pl.program_id(2)
    n_streams = 2 * heads
    row = lax.broadcasted_iota(jnp.int32, (V_HEAD, blk), 0)
    q_maps = []
    for h in range(heads):
        qt = q_ref[0, h * V_HEAD:(h + 1) * V_HEAD, :]
        zero = jnp.zeros_like(qt)
        q_maps += [jnp.where(row < HEAD_DIM, qt, zero), jnp.where(row >= HEAD_DIM, qt, zero)]
    acc_scr[...] = jnp.zeros_like(acc_scr)
    m_scr[...] = jnp.full_like(m_scr, MASK_VALUE)
    l_scr[...] = jnp.zeros_like(l_scr)

    kv_chunk = lax.broadcasted_iota(jnp.int32, (blk, blk), 0) >> CAUSAL_CHUNK_SHIFT
    q_chunk = lax.broadcasted_iota(jnp.int32, (blk, blk), 1) >> CAUSAL_CHUNK_SHIFT

    def scores(j, allowed):
        rows = pl.ds(pl.multiple_of(j * blk, blk), blk)
        for n in range(n_streams):
            h = n // 2
            kt = k_ref[0, rows, h * V_HEAD:(h + 1) * V_HEAD]
            s = jnp.dot(kt, q_maps[n], preferred_element_type=F32)
            if allowed is not None:
                s = jnp.where(allowed, s, MASK_VALUE)
            s_scr[n] = s
            c_scr[n] = jnp.max(s, axis=0, keepdims=True)

    def probs():
        out = []
        for n in range(n_streams):
            m_old = m_scr[n]
            m_new = jnp.maximum(m_old, c_scr[n])
            alpha = jnp.exp2(m_old - m_new)
            p = jnp.exp2(s_scr[n] - m_new)
            l_scr[n] = alpha * l_scr[n] + jnp.sum(p, axis=0, keepdims=True)
            m_scr[n] = m_new
            out.append((alpha, p.astype(BF16)))
        return out

    def accumulate(j, alpha_p):
        cols = pl.ds(pl.multiple_of(j * blk, blk), blk)
        for n, (alpha, p) in enumerate(alpha_p):
            h = n // 2
            vt = v_ref[0, h * V_HEAD:(h + 1) * V_HEAD, cols]
            acc_scr[n] = alpha * acc_scr[n] + jnp.dot(vt, p, preferred_element_type=F32)

    def step(j, next_allowed):
        alpha_p = probs()
        scores(j + 1, next_allowed)
        accumulate(j, alpha_p)

    scores(0, kv_chunk <= q_chunk + qi * (blk >> CAUSAL_CHUNK_SHIFT))

    def body(j, carry):
        step(j, None)
        return carry

    lax.fori_loop(0, qi - 1, body, 0)

    @pl.when(qi >= 1)
    def _():
        step(qi - 1, kv_chunk <= q_chunk)

    accumulate(qi, probs())

    lam_v = lam_ref[...]
    e1 = jnp.exp(jnp.sum(lam_v[0:1] * lam_v[1:2], axis=1, keepdims=True))
    e2 = jnp.exp(jnp.sum(lam_v[2:3] * lam_v[3:4], axis=1, keepdims=True))
    lam = e1 - e2 + lam_init
    for h in range(heads):
        o = (acc_scr[2 * h] / l_scr[2 * h]
             - lam * (acc_scr[2 * h + 1] / l_scr[2 * h + 1]))
        ms = jnp.mean(o * o, axis=0, keepdims=True)
        o = o * lax.rsqrt(ms + NORM_EPS) * hg_ref[...] * (1.0 - lam_init)
        head_rows = slice(h * V_HEAD, (h + 1) * V_HEAD)
        o_ref[0, head_rows, :] = (o * ga_ref[0, head_rows, :].astype(F32)).astype(BF16)


def _diff_attention(lam_vecs, head_g, q_t, k, v_t, ga_t, *, blk, heads, lam_init):
    b, _, s = q_t.shape
    nq = s // blk
    n_streams = 2 * heads
    width = heads * V_HEAD
    head_tile = pl.BlockSpec((1, width, blk), lambda bi, h, i: (bi, h, i))
    return pl.pallas_call(
        functools.partial(_attn_kernel, blk=blk, heads=heads, lam_init=lam_init),
        grid=(b, N_HEADS // heads, nq),
        in_specs=[pl.BlockSpec((4, HEAD_DIM), lambda bi, h, i: (0, 0)),
                  pl.BlockSpec((V_HEAD, 1), lambda bi, h, i: (0, 0)),
                  head_tile,
                  pl.BlockSpec((1, s, width), lambda bi, h, i: (bi, 0, h)),
                  pl.BlockSpec((1, width, s), lambda bi, h, i: (bi, h, 0)),
                  head_tile],
        out_specs=head_tile,
        out_shape=jax.ShapeDtypeStruct((b, D_MODEL, s), BF16),
        scratch_shapes=[pltpu.VMEM((n_streams, V_HEAD, blk), F32),
                        pltpu.VMEM((n_streams, 1, blk), F32),
                        pltpu.VMEM((n_streams, 1, blk), F32),
                        pltpu.VMEM((n_streams, blk, blk), F32),
                        pltpu.VMEM((n_streams, 1, blk), F32)],
        compiler_params=pltpu.CompilerParams(
            dimension_semantics=("parallel", "parallel", "arbitrary"), vmem_limit_bytes=VMEM_LIMIT),
        name="diff_attn",
    )(lam_vecs, head_g, q_t, k, v_t, ga_t)


def _zoh(a_re, a_im, log_dt):
    dt = jnp.exp(log_dt)
    mag = jnp.exp(a_re * dt)
    lre = mag * jnp.cos(a_im * dt)
    lim = mag * jnp.sin(a_im * dt)
    nr = lre - 1.0
    den = a_re * a_re + a_im * a_im
    fre = (nr * a_re + lim * a_im) / den
    fim = (lim * a_re - nr * a_im) / den
    return lre, lim, fre, fim


def _cpow(bre, bim, e, nbits):
    bre = jnp.broadcast_to(bre, e.shape)
    bim = jnp.broadcast_to(bim, e.shape)
    rre = jnp.ones(e.shape, F32)
    rim = jnp.zeros(e.shape, F32)
    for bit in range(nbits):
        take = ((e >> bit) & 1) == 1
        rre, rim = (jnp.where(take, rre * bre - rim * bim, rre),
                    jnp.where(take, rre * bim + rim * bre, rim))
        bre, bim = bre * bre - bim * bim, 2.0 * bre * bim
    return rre, rim


def _ssm_kernel(u_ref, prow_ref, pcol_ref, bre_ref, bim_ref, x1_ref, x2_ref, cre_ref, cim_ref,
                uu_ref, vv_ref, d_ref, y_ref, kt_ref, pq_ref, t_ref, b_ref, yacc_ref, st_ref, *, n_chunks, groups):
    nbc = u_ref.shape[2]
    n_log = SSM_CHUNK.bit_length() - 1
    n_pairs = SSM_GROUP // 2
    lane = lax.broadcasted_iota(jnp.int32, (LANES, LANES), 1)
    sub = lax.broadcasted_iota(jnp.int32, (LANES, LANES), 0)
    causal = lane >= sub

    def setup(g):
        prow = prow_ref[g]
        lre, lim, fre, fim = _zoh(prow[0:1], prow[1:2], prow[2:3])
        pcol = pcol_ref[g]
        lre_c, lim_c, _, _ = _zoh(pcol[:, 0:1], pcol[:, 1:2], pcol[:, 2:3])
        ltre, ltim = _cpow(lre_c, lim_c, lane, n_log)
        lt1re = ltre * lre_c - ltim * lim_c
        lt1im = ltre * lim_c + ltim * lre_c
        lsre, lsim = _cpow(lre, lim, (SSM_CHUNK - 1) - sub, n_log)

        fre_h = fre[:, :SSM_STATE]
        fim_h = fim[:, :SSM_STATE]
        bbre = fre_h * bre_ref[g] - fim_h * bim_ref[g]
        bbim = fre_h * bim_ref[g] + fim_h * bre_ref[g]
        pair = (SSM_GROUP, SSM_GROUP, SSM_STATE)
        flat = (SSM_GROUP * SSM_GROUP, SSM_STATE)
        c_re = jnp.broadcast_to(cre_ref[g][:, None, :], pair).reshape(flat)
        c_im = jnp.broadcast_to(cim_ref[g][:, None, :], pair).reshape(flat)
        b_re = jnp.broadcast_to(bbre[None], pair).reshape(flat)
        b_im = jnp.broadcast_to(bbim[None], pair).reshape(flat)
        mre = c_re * b_re - c_im * b_im
        mim = c_re * b_im + c_im * b_re
        kt_ref[g] = (jnp.dot(mre, ltre[:SSM_STATE], preferred_element_type=F32, precision=lax.Precision.HIGHEST)
                     - jnp.dot(mim, ltim[:SSM_STATE], preferred_element_type=F32, precision=lax.Precision.HIGHEST))
        pq_ref[g, 0] = fre * x1_ref[g] + fim * x2_ref[g]
        pq_ref[g, 1] = fre * x2_ref[g] - fim * x1_ref[g]
        return lre, lim, lsre, lsim, lt1re, lt1im

    tables = [setup(g) for g in range(groups)]

    def build(g, i):
        _, _, lsre, lsim, _, _ = tables[g]
        t_rows, b_rows = [], []
        for dh in range(2):
            hi = 2 * i + dh
            blocks = []
            for ho in range(SSM_GROUP):
                lag = jnp.broadcast_to(kt_ref[g, pl.ds(ho * SSM_GROUP + hi, 1), :], (LANES, LANES))
                toep = pltpu.roll(lag, 0, 1, stride=1, stride_axis=0)
                blocks.append(jnp.where(causal, toep, 0.0).astype(BF16))
            t_rows.append(jnp.concatenate(blocks, axis=1))
            b_rows.append((lsre * pq_ref[g, 0, pl.ds(hi, 1), :] + lsim * pq_ref[g, 1, pl.ds(hi, 1), :]).astype(BF16))
        t_ref[g] = jnp.concatenate(t_rows, axis=0)
        b_ref[g] = jnp.concatenate(b_rows, axis=0)

    for g in range(groups):
        build(g, 0)
    yacc_ref[...] = jnp.zeros_like(yacc_ref)
    st_ref[...] = jnp.zeros_like(st_ref)

    def pair_body(i, carry):
        for g in range(groups):
            x = jnp.concatenate([u_ref[g, 2 * i], u_ref[g, 2 * i + 1]], axis=1)
            yacc_ref[g] += jnp.dot(x, t_ref[g], preferred_element_type=F32)
            st_ref[g] += jnp.dot(x, b_ref[g], preferred_element_type=F32)
        for g in range(groups):
            build(g, jnp.minimum(i + 1, n_pairs - 1))
        return carry

    lax.fori_loop(0, n_pairs, pair_body, 0)

    half_sign = jnp.where(lax.broadcasted_iota(jnp.int32, (1, LANES), 1) < SSM_STATE, -1.0, 1.0)
    chunk_id = lax.broadcasted_iota(jnp.int32, (nbc, LANES), 0) % n_chunks
    for g in range(groups):
        lre, lim, _, _, lt1re, lt1im = tables[g]
        are, aim = lre, lim
        for _ in range(n_log):
            are, aim = are * are - aim * aim, 2.0 * are * aim
        acc = st_ref[g]
        dist = 1
        while dist < n_chunks:
            prev = jnp.where(chunk_id >= dist, pltpu.roll(acc, dist, 0), 0.0)
            acc = acc + are * prev + (aim * half_sign) * pltpu.roll(prev, SSM_STATE, 1)
            are, aim = are * are - aim * aim, 2.0 * are * aim
            dist *= 2
        h_in = jnp.where(chunk_id >= 1, pltpu.roll(acc, 1, 0), 0.0).astype(BF16)

        cm = [(lt1re * uu_ref[g][:, ho:ho + 1] + lt1im * vv_ref[g][:, ho:ho + 1]).astype(BF16)
              for ho in range(SSM_GROUP)]
        y_all = yacc_ref[g] + jnp.dot(h_in, jnp.concatenate(cm, axis=1), preferred_element_type=F32)

        for o in range(SSM_GROUP // SUBLANES):
            ys = [y_all[:, h * LANES:(h + 1) * LANES] + d_ref[g, h] * u_ref[g, h].astype(F32)
                  for h in range(o * SUBLANES, (o + 1) * SUBLANES)]
            y_ref[g, o] = jnp.swapaxes(jnp.stack(ys, axis=0), 0, 1)


def _s5_groups(u_t, a_re, a_im, log_dt, b_re, b_im, c_re, c_im, d_skip, *, n_chunks):
    nbc = u_t.shape[1] // SSM_CHUNK
    u4 = u_t.reshape(SSM_GROUPS, SSM_GROUP, nbc, SSM_CHUNK)
    g = SSM_GROUPS
    dup = lambda v: jnp.concatenate([v, v], axis=-1)
    ldt = jnp.broadcast_to(log_dt[:, None], (g, LANES))
    prow = jnp.concatenate([jnp.stack([dup(a_re), dup(a_im), ldt], axis=1),
                            jnp.zeros((g, 5, LANES), F32)], axis=1)
    pcol = prow.transpose(0, 2, 1)
    bre_t = b_re.transpose(0, 2, 1)
    bim_t = b_im.transpose(0, 2, 1)
    x1 = jnp.concatenate([bre_t, bim_t], axis=-1)
    x2 = jnp.concatenate([-bim_t, bre_t], axis=-1)
    cre_t = c_re.transpose(0, 2, 1)
    cim_t = c_im.transpose(0, 2, 1)
    uu = jnp.concatenate([cre_t, -cim_t], axis=1)
    vv = jnp.concatenate([-cim_t, -cre_t], axis=1)
    d_b = jnp.broadcast_to(d_skip.reshape(g, SSM_GROUP, 1, 1), (g, SSM_GROUP, 1, LANES))

    gps = SSM_GROUPS_PER_STEP

    def per_group(*tail):
        return pl.BlockSpec((gps,) + tail, lambda gi: (gi,) + (0,) * len(tail))

    y4 = pl.pallas_call(
        functools.partial(_ssm_kernel, n_chunks=n_chunks, groups=gps),
        grid=(g // gps,),
        in_specs=[per_group(SSM_GROUP, nbc, SSM_CHUNK), per_group(8, LANES), per_group(LANES, 8),
                  per_group(SSM_GROUP, SSM_STATE), per_group(SSM_GROUP, SSM_STATE),
                  per_group(SSM_GROUP, LANES), per_group(SSM_GROUP, LANES),
                  per_group(SSM_GROUP, SSM_STATE), per_group(SSM_GROUP, SSM_STATE),
                  per_group(LANES, SSM_GROUP), per_group(LANES, SSM_GROUP),
                  per_group(SSM_GROUP, 1, LANES)],
        out_specs=per_group(SSM_GROUP // SUBLANES, nbc, SUBLANES, SSM_CHUNK),
        out_shape=jax.ShapeDtypeStruct((g, SSM_GROUP // SUBLANES, nbc, SUBLANES, SSM_CHUNK), F32),
        scratch_shapes=[pltpu.VMEM((gps, SSM_GROUP * SSM_GROUP, LANES), F32),
                        pltpu.VMEM((gps, 2, SSM_GROUP, LANES), F32),
                        pltpu.VMEM((gps, 2 * SSM_CHUNK, SSM_GROUP * SSM_CHUNK), BF16),
                        pltpu.VMEM((gps, 2 * SSM_CHUNK, LANES), BF16),
                        pltpu.VMEM((gps, nbc, SSM_GROUP * SSM_CHUNK), F32),
                        pltpu.VMEM((gps, nbc, LANES), F32)],
        compiler_params=pltpu.CompilerParams(
            dimension_semantics=("parallel",), vmem_limit_bytes=VMEM_LIMIT),
        name="s5_groups",
    )(u4, prow, pcol, bre_t, bim_t, x1, x2, c_re, c_im, uu, vv, d_b)
    return y4


def _out_kernel(o_ref, y_ref, gs_ref, x_ref, gate_ref, wg_ref, bg_ref, wo_ref, out_ref):
    mixed_attn = jnp.dot(wo_ref[:, :D_MODEL], o_ref[0], preferred_element_type=F32)
    n_chunks = y_ref.shape[2]
    y = jnp.concatenate([y_ref[:, :, c].reshape(D_MODEL, SSM_CHUNK) for c in range(n_chunks)], axis=1)
    z0 = jax.nn.gelu(y)
    t = jnp.dot(wg_ref[...], z0.astype(BF16), preferred_element_type=F32) + bg_ref[...]
    z = z0 * jax.nn.sigmoid(t) * gs_ref[...].astype(F32)
    mixed_t = mixed_attn + jnp.dot(wo_ref[:, D_MODEL:], z.astype(BF16), preferred_element_type=F32)
    out_ref[0] = x_ref[0] + gate_ref[0] * mixed_t.T


def _mix_out(o_t, y4, gs_t, x, gate, w_glu_t, b_glu, w_out_t, *, tn):
    b, s, _ = x.shape
    nt = s // tn
    flat_ch_major = pl.BlockSpec((D_MODEL, tn), lambda bi, i: (0, bi * nt + i))
    tok_major = pl.BlockSpec((1, tn, D_MODEL), lambda bi, i: (bi, i, 0))
    chunk_major = pl.BlockSpec((SSM_GROUPS, SSM_GROUP // SUBLANES, tn // SSM_CHUNK, SUBLANES, SSM_CHUNK),
                               lambda bi, i: (0, 0, bi * nt + i, 0, 0))
    return pl.pallas_call(
        _out_kernel,
        grid=(b, nt),
        in_specs=[pl.BlockSpec((1, D_MODEL, tn), lambda bi, i: (bi, 0, i)),
                  chunk_major, flat_ch_major, tok_major,
                  pl.BlockSpec((1, 1, D_MODEL), lambda bi, i: (bi, 0, 0)),
                  pl.BlockSpec((D_MODEL, D_MODEL), lambda bi, i: (0, 0)),
                  pl.BlockSpec((D_MODEL, 1), lambda bi, i: (0, 0)),
                  pl.BlockSpec((D_MODEL, 2 * D_MODEL), lambda bi, i: (0, 0))],
        out_specs=tok_major,
        out_shape=jax.ShapeDtypeStruct(x.shape, x.dtype),
        compiler_params=pltpu.CompilerParams(
            dimension_semantics=("parallel", "parallel"), vmem_limit_bytes=VMEM_LIMIT),
        name="mix_out",
    )(o_t, y4, gs_t, x, gate, w_glu_t, b_glu, w_out_t)


def kernel(x, c, w_ada, b_ada, norm_g, w_in, q_norm_g, k_norm_g, lam_q1, lam_k1, lam_q2, lam_k2,
           head_norm_g, ssm_a_re, ssm_a_im, ssm_log_dt, ssm_b_re, ssm_b_im, ssm_c_re, ssm_c_im,
           ssm_d, w_glu, b_glu, w_out):
    b, s, _ = x.shape
    depth = w_in.shape[0]
    tok_tile = min(512, s)
    attn_tile = min(512, s)
    assert s % tok_tile == 0 and s % attn_tile == 0 and s % SSM_CHUNK == 0 and tok_tile % LANES == 0

    pos = jnp.arange(s, dtype=F32)
    inv_freq = 1.0 / (ROPE_THETA ** (jnp.arange(0, HEAD_DIM, 2, dtype=F32) / HEAD_DIM))
    ang_t = inv_freq[:, None] * pos[None, :]
    cos_t, sin_t = jnp.cos(ang_t), jnp.sin(ang_t)

    for l in range(depth):
        lam_init = 0.8 - 0.6 * math.exp(-0.3 * l)
        mod = _ada_mod(c, w_ada[l], b_ada[l])
        shift, scale, gate = (mod[:, i * D_MODEL:(i + 1) * D_MODEL].reshape(b, 1, D_MODEL) for i in range(3))
        q_t, k, v_t, ga_t, u_t, gs_t = _in_proj(
            x, shift, scale, norm_g[l].reshape(1, D_MODEL), w_in[l].T.astype(BF16), cos_t, sin_t,
            q_norm_g[l].reshape(HEAD_DIM, 1), k_norm_g[l].reshape(HEAD_DIM, 1), tm=tok_tile)
        lam_vecs = jnp.stack([lam_q1[l], lam_k1[l], lam_q2[l], lam_k2[l]]).astype(F32)
        o_t = _diff_attention(lam_vecs, head_norm_g[l].reshape(V_HEAD, 1), q_t, k, v_t, ga_t,
                              blk=attn_tile, heads=ATTN_HEADS_PER_STEP, lam_init=lam_init)
        y4 = _s5_groups(u_t, ssm_a_re[l], ssm_a_im[l], ssm_log_dt[l], ssm_b_re[l], ssm_b_im[l],
                         ssm_c_re[l], ssm_c_im[l], ssm_d[l], n_chunks=s // SSM_CHUNK)
        x = _mix_out(o_t, y4, gs_t, x, gate, w_glu[l].T.astype(BF16), b_glu[l].reshape(D_MODEL, 1),
                     w_out[l].T.astype(BF16), tn=tok_tile)
    return x
```

```python
import functools
import math

import jax
import jax.numpy as jnp
from jax import lax
from jax.experimental import pallas as pl
from jax.experimental.pallas import tpu as pltpu

F32 = jnp.float32
BF16 = jnp.bfloat16

D_MODEL = 1024
HEAD_DIM = 64
HALF = HEAD_DIM // 2
V_HEAD = 2 * HEAD_DIM
N_HEADS = D_MODEL // V_HEAD
SSM_GROUPS = 64
SSM_GROUP = 16
SSM_STATE = 64
N_SECTIONS = 6
ROPE_THETA = 10000.0
NORM_EPS = 1e-6
MASK_VALUE = -1e30
CAUSAL_CHUNK_SHIFT = 6
LANES = 128
SUBLANES = 8
SSM_CHUNK = LANES
VMEM_LIMIT = 56 * 1024 * 1024
ATTN_HEADS_PER_STEP = 4
SSM_GROUPS_PER_STEP = 2


def _silu(v):
    return v * jax.nn.sigmoid(v)


def _ada_kernel(c_ref, w_ref, b_ref, o_ref):
    c = c_ref[...]
    o_ref[...] = jnp.dot(_silu(c), w_ref[...], preferred_element_type=F32) + b_ref[...]


def _ada_mod(c, w_ada, b_ada):
    b = c.shape[0]
    return pl.pallas_call(
        _ada_kernel,
        grid=(3,),
        in_specs=[pl.BlockSpec((b, D_MODEL), lambda j: (0, 0)),
                  pl.BlockSpec((D_MODEL, D_MODEL), lambda j: (0, j)),
                  pl.BlockSpec((1, D_MODEL), lambda j: (0, j))],
        out_specs=pl.BlockSpec((b, D_MODEL), lambda j: (0, j)),
        out_shape=jax.ShapeDtypeStruct((b, 3 * D_MODEL), F32),
        name="ada_mod",
    )(c, w_ada, b_ada.reshape(1, 3 * D_MODEL))


def _inproj_kernel(x_ref, shift_ref, scale_ref, ng_ref, wt_ref, cos_ref, sin_ref, qg_ref, kg_ref,
                   q_ref, k_ref, v_ref, ga_ref, u_ref, gs_ref, *, q_mult):
    tm = x_ref.shape[1]
    x = x_ref[0]
    ms = jnp.mean(x * x, axis=-1, keepdims=True)
    y = x * lax.rsqrt(ms + NORM_EPS) * ng_ref[...]
    h = (y * (1.0 + scale_ref[0]) + shift_ref[0]).astype(BF16)
    cos = cos_ref[...][None]
    sin = sin_ref[...][None]

    def proj(sec):
        w = wt_ref[sec * D_MODEL:(sec + 1) * D_MODEL, :]
        return lax.dot_general(w, h, (((1,), (1,)), ((), ())), preferred_element_type=F32)

    def norm_rope(r, g_ref):
        r3 = r.reshape(2 * N_HEADS, HEAD_DIM, tm)
        ss = jnp.mean(r3 * r3, axis=1, keepdims=True)
        yn = r3 * lax.rsqrt(ss + NORM_EPS) * g_ref[...][None]
        x1 = yn[:, :HALF, :]
        x2 = yn[:, HALF:, :]
        o = jnp.concatenate([x1 * cos - x2 * sin, x2 * cos + x1 * sin], axis=1)
        return o.reshape(D_MODEL, tm)

    q_ref[0] = (norm_rope(proj(0), qg_ref) * q_mult).astype(BF16)
    k_ref[0] = norm_rope(proj(1), kg_ref).T.astype(BF16)
    v_ref[0] = proj(2).astype(BF16)
    ga_ref[0] = _silu(proj(3)).astype(BF16)
    u_ref[...] = proj(4).astype(BF16)
    gs_ref[...] = _silu(proj(5)).astype(BF16)


def _in_proj(x, shift, scale, norm_g, w_in_t, cos_t, sin_t, q_g, k_g, *, tm):
    b, s, _ = x.shape
    nt = s // tm
    q_mult = HEAD_DIM ** -0.5 * math.log2(math.e)
    tok_major = pl.BlockSpec((1, tm, D_MODEL), lambda bi, i: (bi, i, 0))
    ch_major = pl.BlockSpec((1, D_MODEL, tm), lambda bi, i: (bi, 0, i))
    flat_ch_major = pl.BlockSpec((D_MODEL, tm), lambda bi, i: (0, bi * nt + i))
    per_batch = pl.BlockSpec((1, 1, D_MODEL), lambda bi, i: (bi, 0, 0))
    rope_tab = pl.BlockSpec((HALF, tm), lambda bi, i: (0, i))
    col64 = pl.BlockSpec((HEAD_DIM, 1), lambda bi, i: (0, 0))
    return pl.pallas_call(
        functools.partial(_inproj_kernel, q_mult=q_mult),
        grid=(b, nt),
        in_specs=[tok_major, per_batch, per_batch,
                  pl.BlockSpec((1, D_MODEL), lambda bi, i: (0, 0)),
                  pl.BlockSpec((N_SECTIONS * D_MODEL, D_MODEL), lambda bi, i: (0, 0),
                               pipeline_mode=pl.Buffered(1)),
                  rope_tab, rope_tab, col64, col64],
        out_specs=[ch_major, tok_major, ch_major, ch_major, flat_ch_major, flat_ch_major],
        out_shape=[jax.ShapeDtypeStruct((b, D_MODEL, s), BF16),
                   jax.ShapeDtypeStruct((b, s, D_MODEL), BF16),
                   jax.ShapeDtypeStruct((b, D_MODEL, s), BF16),
                   jax.ShapeDtypeStruct((b, D_MODEL, s), BF16),
                   jax.ShapeDtypeStruct((D_MODEL, b * s), BF16),
                   jax.ShapeDtypeStruct((D_MODEL, b * s), BF16)],
        compiler_params=pltpu.CompilerParams(
            dimension_semantics=("parallel", "parallel"), vmem_limit_bytes=VMEM_LIMIT),
        name="in_proj",
    )(x, shift, scale, norm_g, w_in_t, cos_t, sin_t, q_g, k_g)


def _attn_kernel(lam_ref, hg_ref, q_ref, k_ref, v_ref, ga_ref, o_ref,
                 acc_scr, m_scr, l_scr, s_scr, c_scr, *, blk, heads, lam_init):
    qi = pl.program_id(2)
    n_streams = 2 * heads
    row = lax.broadcasted_iota(jnp.int32, (V_HEAD, blk), 0)
    q_maps = []
    for h in range(heads):
        qt = q_ref[0, h * V_HEAD:(h + 1) * V_HEAD, :]
        zero = jnp.zeros_like(qt)
        q_maps += [jnp.where(row < HEAD_DIM, qt, zero), jnp.where(row >= HEAD_DIM, qt, zero)]
    acc_scr[...] = jnp.zeros_like(acc_scr)
    m_scr[...] = jnp.full_like(m_scr, MASK_VALUE)
    l_scr[...] = jnp.zeros_like(l_scr)

    kv_chunk = lax.broadcasted_iota(jnp.int32, (blk, blk), 0) >> CAUSAL_CHUNK_SHIFT
    q_chunk = lax.broadcasted_iota(jnp.int32, (blk, blk), 1) >> CAUSAL_CHUNK_SHIFT

    def scores(j, allowed):
        rows = pl.ds(pl.multiple_of(j * blk, blk), blk)
        for n in range(n_streams):
            h = n // 2
            kt = k_ref[0, rows, h * V_HEAD:(h + 1) * V_HEAD]
            s = jnp.dot(kt, q_maps[n], preferred_element_type=F32)
            if allowed is not None:
                s = jnp.where(allowed, s, MASK_VALUE)
            s_scr[n] = s
            c_scr[n] = jnp.max(s, axis=0, keepdims=True)

    def probs():
        out = []
        for n in range(n_streams):
            m_old = m_scr[n]
            m_new = jnp.maximum(m_old, c_scr[n])
            alpha = jnp.exp2(m_old - m_new)
            p = jnp.exp2(s_scr[n] - m_new)
            l_scr[n] = alpha * l_scr[n] + jnp.sum(p, axis=0, keepdims=True)
            m_scr[n] = m_new
            out.append((alpha, p.astype(BF16)))
        return out

    def accumulate(j, alpha_p):
        cols = pl.ds(pl.multiple_of(j * blk, blk), blk)
        for n, (alpha, p) in enumerate(alpha_p):
            h = n // 2
            vt = v_ref[0, h * V_HEAD:(h + 1) * V_HEAD, cols]
            acc_scr[n] = alpha * acc_scr[n] + jnp.dot(vt, p, preferred_element_type=F32)

    def step(j, next_allowed):
        alpha_p = probs()
        scores(j + 1, next_allowed)
        accumulate(j, alpha_p)

    scores(0, kv_chunk <= q_chunk + qi * (blk >> CAUSAL_CHUNK_SHIFT))

    def body(j, carry):
        step(j, None)
        return carry

    lax.fori_loop(0, qi - 1, body, 0)

    @pl.when(qi >= 1)
    def _():
        step(qi - 1, kv_chunk <= q_chunk)

    accumulate(qi, probs())

    lam_v = lam_ref[...]
    e1 = jnp.exp(jnp.sum(lam_v[0:1] * lam_v[1:2], axis=1, keepdims=True))
    e2 = jnp.exp(jnp.sum(lam_v[2:3] * lam_v[3:4], axis=1, keepdims=True))
    lam = e1 - e2 + lam_init
    for h in range(heads):
        o = (acc_scr[2 * h] / l_scr[2 * h]
             - lam * (acc_scr[2 * h + 1] / l_scr[2 * h + 1]))
        ms = jnp.mean(o * o, axis=0, keepdims=True)
        o = o * lax.rsqrt(ms + NORM_EPS) * hg_ref[...] * (1.0 - lam_init)
        head_rows = slice(h * V_HEAD, (h + 1) * V_HEAD)
        o_ref[0, head_rows, :] = (o * ga_ref[0, head_rows, :].astype(F32)).astype(BF16)


def _diff_attention(lam_vecs, head_g, q_t, k, v_t, ga_t, *, blk, heads, lam_init):
    b, _, s = q_t.shape
    nq = s // blk
    n_streams = 2 * heads
    width = heads * V_HEAD
    head_tile = pl.BlockSpec((1, width, blk), lambda bi, h, i: (bi, h, i))
    return pl.pallas_call(
        functools.partial(_attn_kernel, blk=blk, heads=heads, lam_init=lam_init),
        grid=(b, N_HEADS // heads, nq),
        in_specs=[pl.BlockSpec((4, HEAD_DIM), lambda bi, h, i: (0, 0)),
                  pl.BlockSpec((V_HEAD, 1), lambda bi, h, i: (0, 0)),
                  head_tile,
                  pl.BlockSpec((1, s, width), lambda bi, h, i: (bi, 0, h)),
                  pl.BlockSpec((1, width, s), lambda bi, h, i: (bi, h, 0)),
                  head_tile],
        out_specs=head_tile,
        out_shape=jax.ShapeDtypeStruct((b, D_MODEL, s), BF16),
        scratch_shapes=[pltpu.VMEM((n_streams, V_HEAD, blk), F32),
                        pltpu.VMEM((n_streams, 1, blk), F32),
                        pltpu.VMEM((n_streams, 1, blk), F32),
                        pltpu.VMEM((n_streams, blk, blk), F32),
                        pltpu.VMEM((n_streams, 1, blk), F32)],
        compiler_params=pltpu.CompilerParams(
            dimension_semantics=("parallel", "parallel", "arbitrary"), vmem_limit_bytes=VMEM_LIMIT),
        name="diff_attn",
    )(lam_vecs, head_g, q_t, k, v_t, ga_t)


def _zoh(a_re, a_im, log_dt):
    dt = jnp.exp(log_dt)
    mag = jnp.exp(a_re * dt)
    lre = mag * jnp.cos(a_im * dt)
    lim = mag * jnp.sin(a_im * dt)
    nr = lre - 1.0
    den = a_re * a_re + a_im * a_im
    fre = (nr * a_re + lim * a_im) / den
    fim = (lim * a_re - nr * a_im) / den
    return lre, lim, fre, fim


def _cpow(bre, bim, e, nbits):
    bre = jnp.broadcast_to(bre, e.shape)
    bim = jnp.broadcast_to(bim, e.shape)
    rre = jnp.ones(e.shape, F32)
    rim = jnp.zeros(e.shape, F32)
    for bit in range(nbits):
        take = ((e >> bit) & 1) == 1
        rre, rim = (jnp.where(take, rre * bre - rim * bim, rre),
                    jnp.where(take, rre * bim + rim * bre, rim))
        bre, bim = bre * bre - bim * bim, 2.0 * bre * bim
    return rre, rim


def _ssm_kernel(u_ref, prow_ref, pcol_ref, bre_ref, bim_ref, x1_ref, x2_ref, cre_ref, cim_ref,
                uu_ref, vv_ref, d_ref, y_ref, kt_ref, pq_ref, t_ref, b_ref, yacc_ref, st_ref, *, n_chunks, groups):
    nbc = u_ref.shape[2]
    n_log = SSM_CHUNK.bit_length() - 1
    n_pairs = SSM_GROUP // 2
    lane = lax.broadcasted_iota(jnp.int32, (LANES, LANES), 1)
    sub = lax.broadcasted_iota(jnp.int32, (LANES, LANES), 0)
    causal = lane >= sub

    def setup(g):
        prow = prow_ref[g]
        lre, lim, fre, fim = _zoh(prow[0:1], prow[1:2], prow[2:3])
        pcol = pcol_ref[g]
        lre_c, lim_c, _, _ = _zoh(pcol[:, 0:1], pcol[:, 1:2], pcol[:, 2:3])
        ltre, ltim = _cpow(lre_c, lim_c, lane, n_log)
        lt1re = ltre * lre_c - ltim * lim_c
        lt1im = ltre * lim_c + ltim * lre_c
        lsre, lsim = _cpow(lre, lim, (SSM_CHUNK - 1) - sub, n_log)

        fre_h = fre[:, :SSM_STATE]
        fim_h = fim[:, :SSM_STATE]
        bbre = fre_h * bre_ref[g] - fim_h * bim_ref[g]
        bbim = fre_h * bim_ref[g] + fim_h * bre_ref[g]
        pair = (SSM_GROUP, SSM_GROUP, SSM_STATE)
        flat = (SSM_GROUP * SSM_GROUP, SSM_STATE)
        c_re = jnp.broadcast_to(cre_ref[g][:, None, :], pair).reshape(flat)
        c_im = jnp.broadcast_to(cim_ref[g][:, None, :], pair).reshape(flat)
        b_re = jnp.broadcast_to(bbre[None], pair).reshape(flat)
        b_im = jnp.broadcast_to(bbim[None], pair).reshape(flat)
        mre = c_re * b_re - c_im * b_im
        mim = c_re * b_im + c_im * b_re
        kt_ref[g] = (jnp.dot(mre, ltre[:SSM_STATE], preferred_element_type=F32, precision=lax.Precision.HIGHEST)
                     - jnp.dot(mim, ltim[:SSM_STATE], preferred_element_type=F32, precision=lax.Precision.HIGHEST))
        pq_ref[g, 0] = fre * x1_ref[g] + fim * x2_ref[g]
        pq_ref[g, 1] = fre * x2_ref[g] - fim * x1_ref[g]
        return lre, lim, lsre, lsim, lt1re, lt1im

    tables = [setup(g) for g in range(groups)]

    def build(g, i):
        _, _, lsre, lsim, _, _ = tables[g]
        t_rows, b_rows = [], []
        for dh in range(2):
            hi = 2 * i + dh
            blocks = []
            for ho in range(SSM_GROUP):
                lag = jnp.broadcast_to(kt_ref[g, pl.ds(ho * SSM_GROUP + hi, 1), :], (LANES, LANES))
                toep = pltpu.roll(lag, 0, 1, stride=1, stride_axis=0)
                blocks.append(jnp.where(causal, toep, 0.0).astype(BF16))
            t_rows.append(jnp.concatenate(blocks, axis=1))
            b_rows.append((lsre * pq_ref[g, 0, pl.ds(hi, 1), :] + lsim * pq_ref[g, 1, pl.ds(hi, 1), :]).astype(BF16))
        t_ref[g] = jnp.concatenate(t_rows, axis=0)
        b_ref[g] = jnp.concatenate(b_rows, axis=0)

    for g in range(groups):
        build(g, 0)
    yacc_ref[...] = jnp.zeros_like(yacc_ref)
    st_ref[...] = jnp.zeros_like(st_ref)

    def pair_body(i, carry):
        for g in range(groups):
            x = jnp.concatenate([u_ref[g, 2 * i], u_ref[g, 2 * i + 1]], axis=1)
            yacc_ref[g] += jnp.dot(x, t_ref[g], preferred_element_type=F32)
            st_ref[g] += jnp.dot(x, b_ref[g], preferred_element_type=F32)
        for g in range(groups):
            build(g, jnp.minimum(i + 1, n_pairs - 1))
        return carry

    lax.fori_loop(0, n_pairs, pair_body, 0)

    half_sign = jnp.where(lax.broadcasted_iota(jnp.int32, (1, LANES), 1) < SSM_STATE, -1.0, 1.0)
    chunk_id = lax.broadcasted_iota(jnp.int32, (nbc, LANES), 0) % n_chunks
    for g in range(groups):
        lre, lim, _, _, lt1re, lt1im = tables[g]
        are, aim = lre, lim
        for _ in range(n_log):
            are, aim = are * are - aim * aim, 2.0 * are * aim
        acc = st_ref[g]
        dist = 1
        while dist < n_chunks:
            prev = jnp.where(chunk_id >= dist, pltpu.roll(acc, dist, 0), 0.0)
            acc = acc + are * prev + (aim * half_sign) * pltpu.roll(prev, SSM_STATE, 1)
            are, aim = are * are - aim * aim, 2.0 * are * aim
            dist *= 2
        h_in = jnp.where(chunk_id >= 1, pltpu.roll(acc, 1, 0), 0.0).astype(BF16)

        cm = [(lt1re * uu_ref[g][:, ho:ho + 1] + lt1im * vv_ref[g][:, ho:ho + 1]).astype(BF16)
              for ho in range(SSM_GROUP)]
        y_all = yacc_ref[g] + jnp.dot(h_in, jnp.concatenate(cm, axis=1), preferred_element_type=F32)

        for o in range(SSM_GROUP // SUBLANES):
            ys = [y_all[:, h * LANES:(h + 1) * LANES] + d_ref[g, h] * u_ref[g, h].astype(F32)
                  for h in range(o * SUBLANES, (o + 1) * SUBLANES)]
            y_ref[g, o] = jnp.swapaxes(jnp.stack(ys, axis=0), 0, 1)


def _s5_groups(u_t, a_re, a_im, log_dt, b_re, b_im, c_re, c_im, d_skip, *, n_chunks):
    nbc = u_t.shape[1] // SSM_CHUNK
    u4 = u_t.reshape(SSM_GROUPS, SSM_GROUP, nbc, SSM_CHUNK)
    g = SSM_GROUPS
    dup = lambda v: jnp.concatenate([v, v], axis=-1)
    ldt = jnp.broadcast_to(log_dt[:, None], (g, LANES))
    prow = jnp.concatenate([jnp.stack([dup(a_re), dup(a_im), ldt], axis=1),
                            jnp.zeros((g, 5, LANES), F32)], axis=1)
    pcol = prow.transpose(0, 2, 1)
    bre_t = b_re.transpose(0, 2, 1)
    bim_t = b_im.transpose(0, 2, 1)
    x1 = jnp.concatenate([bre_t, bim_t], axis=-1)
    x2 = jnp.concatenate([-bim_t, bre_t], axis=-1)
    cre_t = c_re.transpose(0, 2, 1)
    cim_t = c_im.transpose(0, 2, 1)
    uu = jnp.concatenate([cre_t, -cim_t], axis=1)
    vv = jnp.concatenate([-cim_t, -cre_t], axis=1)
    d_b = jnp.broadcast_to(d_skip.reshape(g, SSM_GROUP, 1, 1), (g, SSM_GROUP, 1, LANES))

    gps = SSM_GROUPS_PER_STEP

    def per_group(*tail):
        return pl.BlockSpec((gps,) + tail, lambda gi: (gi,) + (0,) * len(tail))

    y4 = pl.pallas_call(
        functools.partial(_ssm_kernel, n_chunks=n_chunks, groups=gps),
        grid=(g // gps,),
        in_specs=[per_group(SSM_GROUP, nbc, SSM_CHUNK), per_group(8, LANES), per_group(LANES, 8),
                  per_group(SSM_GROUP, SSM_STATE), per_group(SSM_GROUP, SSM_STATE),
                  per_group(SSM_GROUP, LANES), per_group(SSM_GROUP, LANES),
                  per_group(SSM_GROUP, SSM_STATE), per_group(SSM_GROUP, SSM_STATE),
                  per_group(LANES, SSM_GROUP), per_group(LANES, SSM_GROUP),
                  per_group(SSM_GROUP, 1, LANES)],
        out_specs=per_group(SSM_GROUP // SUBLANES, nbc, SUBLANES, SSM_CHUNK),
        out_shape=jax.ShapeDtypeStruct((g, SSM_GROUP // SUBLANES, nbc, SUBLANES, SSM_CHUNK), F32),
        scratch_shapes=[pltpu.VMEM((gps, SSM_GROUP * SSM_GROUP, LANES), F32),
                        pltpu.VMEM((gps, 2, SSM_GROUP, LANES), F32),
                        pltpu.VMEM((gps, 2 * SSM_CHUNK, SSM_GROUP * SSM_CHUNK), BF16),
                        pltpu.VMEM((gps, 2 * SSM_CHUNK, LANES), BF16),
                        pltpu.VMEM((gps, nbc, SSM_GROUP * SSM_CHUNK), F32),
                        pltpu.VMEM((gps, nbc, LANES), F32)],
        compiler_params=pltpu.CompilerParams(
            dimension_semantics=("parallel",), vmem_limit_bytes=VMEM_LIMIT),
        name="s5_groups",
    )(u4, prow, pcol, bre_t, bim_t, x1, x2, c_re, c_im, uu, vv, d_b)
    return y4


def _out_kernel(o_ref, y_ref, gs_ref, x_ref, gate_ref, wg_ref, bg_ref, wo_ref, out_ref):
    mixed_attn = jnp.dot(wo_ref[:, :D_MODEL], o_ref[0], preferred_element_type=F32)
    n_chunks = y_ref.shape[2]
    y = jnp.concatenate([y_ref[:, :, c].reshape(D_MODEL, SSM_CHUNK) for c in range(n_chunks)], axis=1)
    z0 = jax.nn.gelu(y)
    t = jnp.dot(wg_ref[...], z0.astype(BF16), preferred_element_type=F32) + bg_ref[...]
    z = z0 * jax.nn.sigmoid(t) * gs_ref[...].astype(F32)
    mixed_t = mixed_attn + jnp.dot(wo_ref[:, D_MODEL:], z.astype(BF16), preferred_element_type=F32)
    out_ref[0] = x_ref[0] + gate_ref[0] * mixed_t.T


def _mix_out(o_t, y4, gs_t, x, gate, w_glu_t, b_glu, w_out_t, *, tn):
    b, s, _ = x.shape
    nt = s // tn
    flat_ch_major = pl.BlockSpec((D_MODEL, tn), lambda bi, i: (0, bi * nt + i))
    tok_major = pl.BlockSpec((1, tn, D_MODEL), lambda bi, i: (bi, i, 0))
    chunk_major = pl.BlockSpec((SSM_GROUPS, SSM_GROUP // SUBLANES, tn // SSM_CHUNK, SUBLANES, SSM_CHUNK),
                               lambda bi, i: (0, 0, bi * nt + i, 0, 0))
    return pl.pallas_call(
        _out_kernel,
        grid=(b, nt),
        in_specs=[pl.BlockSpec((1, D_MODEL, tn), lambda bi, i: (bi, 0, i)),
                  chunk_major, flat_ch_major, tok_major,
                  pl.BlockSpec((1, 1, D_MODEL), lambda bi, i: (bi, 0, 0)),
                  pl.BlockSpec((D_MODEL, D_MODEL), lambda bi, i: (0, 0)),
                  pl.BlockSpec((D_MODEL, 1), lambda bi, i: (0, 0)),
                  pl.BlockSpec((D_MODEL, 2 * D_MODEL), lambda bi, i: (0, 0))],
        out_specs=tok_major,
        out_shape=jax.ShapeDtypeStruct(x.shape, x.dtype),
        compiler_params=pltpu.CompilerParams(
            dimension_semantics=("parallel", "parallel"), vmem_limit_bytes=VMEM_LIMIT),
        name="mix_out",
    )(o_t, y4, gs_t, x, gate, w_glu_t, b_glu, w_out_t)


def kernel(x, c, w_ada, b_ada, norm_g, w_in, q_norm_g, k_norm_g, lam_q1, lam_k1, lam_q2, lam_k2,
           head_norm_g, ssm_a_re, ssm_a_im, ssm_log_dt, ssm_b_re, ssm_b_im, ssm_c_re, ssm_c_im,
           ssm_d, w_glu, b_glu, w_out):
    b, s, _ = x.shape
    depth = w_in.shape[0]
    tok_tile = min(512, s)
    attn_tile = min(512, s)
    assert s % tok_tile == 0 and s % attn_tile == 0 and s % SSM_CHUNK == 0 and tok_tile % LANES == 0

    pos = jnp.arange(s, dtype=F32)
    inv_freq = 1.0 / (ROPE_THETA ** (jnp.arange(0, HEAD_DIM, 2, dtype=F32) / HEAD_DIM))
    ang_t = inv_freq[:, None] * pos[None, :]
    cos_t, sin_t = jnp.cos(ang_t), jnp.sin(ang_t)

    for l in range(depth):
        lam_init = 0.8 - 0.6 * math.exp(-0.3 * l)
        mod = _ada_mod(c, w_ada[l], b_ada[l])
        shift, scale, gate = (mod[:, i * D_MODEL:(i + 1) * D_MODEL].reshape(b, 1, D_MODEL) for i in range(3))
        q_t, k, v_t, ga_t, u_t, gs_t = _in_proj(
            x, shift, scale, norm_g[l].reshape(1, D_MODEL), w_in[l].T.astype(BF16), cos_t, sin_t,
            q_norm_g[l].reshape(HEAD_DIM, 1), k_norm_g[l].reshape(HEAD_DIM, 1), tm=min(2 * tok_tile, s))
        lam_vecs = jnp.stack([lam_q1[l], lam_k1[l], lam_q2[l], lam_k2[l]]).astype(F32)
        o_t = _diff_attention(lam_vecs, head_norm_g[l].reshape(V_HEAD, 1), q_t, k, v_t, ga_t,
                              blk=attn_tile, heads=ATTN_HEADS_PER_STEP, lam_init=lam_init)
        y4 = _s5_groups(u_t, ssm_a_re[l], ssm_a_im[l], ssm_log_dt[l], ssm_b_re[l], ssm_b_im[l],
                         ssm_c_re[l], ssm_c_im[l], ssm_d[l], n_chunks=s // SSM_CHUNK)
        x = _mix_out(o_t, y4, gs_t, x, gate, w_glu[l].T.astype(BF16), b_glu[l].reshape(D_MODEL, 1),
                     w_out[l].T.astype(BF16), tn=tok_tile)
    return x
```

```python
import functools
import math

import jax
import jax.numpy as jnp
from jax import lax
from jax.experimental import pallas as pl
from jax.experimental.pallas import tpu as pltpu

F32 = jnp.float32
BF16 = jnp.bfloat16

D_MODEL = 1024
HEAD_DIM = 64
HALF = HEAD_DIM // 2
V_HEAD = 2 * HEAD_DIM
N_HEADS = D_MODEL // V_HEAD
SSM_GROUPS = 64
SSM_GROUP = 16
SSM_STATE = 64
N_SECTIONS = 6
ROPE_THETA = 10000.0
NORM_EPS = 1e-6
MASK_VALUE = -1e30
CAUSAL_CHUNK_SHIFT = 6
LANES = 128
SUBLANES = 8
SSM_CHUNK = LANES
VMEM_LIMIT = 56 * 1024 * 1024
ATTN_HEADS_PER_STEP = 4
SSM_GROUPS_PER_STEP = 2


def _silu(v):
    return v * jax.nn.sigmoid(v)


def _ada_kernel(c_ref, w_ref, b_ref, o_ref):
    c = c_ref[...]
    o_ref[...] = jnp.dot(_silu(c), w_ref[...], preferred_element_type=F32) + b_ref[...]


def _ada_mod(c, w_ada, b_ada):
    b = c.shape[0]
    return pl.pallas_call(
        _ada_kernel,
        grid=(3,),
        in_specs=[pl.BlockSpec((b, D_MODEL), lambda j: (0, 0)),
                  pl.BlockSpec((D_MODEL, D_MODEL), lambda j: (0, j)),
                  pl.BlockSpec((1, D_MODEL), lambda j: (0, j))],
        out_specs=pl.BlockSpec((b, D_MODEL), lambda j: (0, j)),
        out_shape=jax.ShapeDtypeStruct((b, 3 * D_MODEL), F32),
        name="ada_mod",
    )(c, w_ada, b_ada.reshape(1, 3 * D_MODEL))


def _inproj_kernel(x_ref, shift_ref, scale_ref, ng_ref, wt_ref, cos_ref, sin_ref, qg_ref, kg_ref,
                   q_ref, k_ref, v_ref, ga_ref, u_ref, gs_ref, *, q_mult):
    tm = x_ref.shape[1]
    x = x_ref[0]
    ms = jnp.mean(x * x, axis=-1, keepdims=True)
    y = x * lax.rsqrt(ms + NORM_EPS) * ng_ref[...]
    h = (y * (1.0 + scale_ref[0]) + shift_ref[0]).astype(BF16)
    cos = cos_ref[...][None]
    sin = sin_ref[...][None]

    def proj(sec):
        w = wt_ref[sec * D_MODEL:(sec + 1) * D_MODEL, :]
        return lax.dot_general(w, h, (((1,), (1,)), ((), ())), preferred_element_type=F32)

    def norm_rope(r, g_ref):
        r3 = r.reshape(2 * N_HEADS, HEAD_DIM, tm)
        ss = jnp.mean(r3 * r3, axis=1, keepdims=True)
        yn = r3 * lax.rsqrt(ss + NORM_EPS) * g_ref[...][None]
        x1 = yn[:, :HALF, :]
        x2 = yn[:, HALF:, :]
        o = jnp.concatenate([x1 * cos - x2 * sin, x2 * cos + x1 * sin], axis=1)
        return o.reshape(D_MODEL, tm)

    q_ref[0] = (norm_rope(proj(0), qg_ref) * q_mult).astype(BF16)
    k_ref[0] = norm_rope(proj(1), kg_ref).T.astype(BF16)
    v_ref[0] = proj(2).astype(BF16)
    ga_ref[0] = _silu(proj(3)).astype(BF16)
    u_ref[...] = proj(4).astype(BF16)
    gs_ref[...] = _silu(proj(5)).astype(BF16)


def _in_proj(x, shift, scale, norm_g, w_in_t, cos_t, sin_t, q_g, k_g, *, tm):
    b, s, _ = x.shape
    nt = s // tm
    q_mult = HEAD_DIM ** -0.5 * math.log2(math.e)
    tok_major = pl.BlockSpec((1, tm, D_MODEL), lambda bi, i: (bi, i, 0))
    ch_major = pl.BlockSpec((1, D_MODEL, tm), lambda bi, i: (bi, 0, i))
    flat_ch_major = pl.BlockSpec((D_MODEL, tm), lambda bi, i: (0, bi * nt + i))
    per_batch = pl.BlockSpec((1, 1, D_MODEL), lambda bi, i: (bi, 0, 0))
    rope_tab = pl.BlockSpec((HALF, tm), lambda bi, i: (0, i))
    col64 = pl.BlockSpec((HEAD_DIM, 1), lambda bi, i: (0, 0))
    return pl.pallas_call(
        functools.partial(_inproj_kernel, q_mult=q_mult),
        grid=(b, nt),
        in_specs=[tok_major, per_batch, per_batch,
                  pl.BlockSpec((1, D_MODEL), lambda bi, i: (0, 0)),
                  pl.BlockSpec((N_SECTIONS * D_MODEL, D_MODEL), lambda bi, i: (0, 0),
                               pipeline_mode=pl.Buffered(1)),
                  rope_tab, rope_tab, col64, col64],
        out_specs=[ch_major, tok_major, ch_major, ch_major, flat_ch_major, flat_ch_major],
        out_shape=[jax.ShapeDtypeStruct((b, D_MODEL, s), BF16),
                   jax.ShapeDtypeStruct((b, s, D_MODEL), BF16),
                   jax.ShapeDtypeStruct((b, D_MODEL, s), BF16),
                   jax.ShapeDtypeStruct((b, D_MODEL, s), BF16),
                   jax.ShapeDtypeStruct((D_MODEL, b * s), BF16),
                   jax.ShapeDtypeStruct((D_MODEL, b * s), BF16)],
        compiler_params=pltpu.CompilerParams(
            dimension_semantics=("parallel", "parallel"), vmem_limit_bytes=VMEM_LIMIT),
        name="in_proj",
    )(x, shift, scale, norm_g, w_in_t, cos_t, sin_t, q_g, k_g)


def _attn_kernel(lam_ref, hg_ref, q_ref, k_ref, v_ref, ga_ref, o_ref,
                 acc_scr, m_scr, l_scr, s_scr, c_scr, *, blk, heads, lam_init):
    qi = pl.program_id(2)
    n_streams = 2 * heads
    row = lax.broadcasted_iota(jnp.int32, (V_HEAD, blk), 0)
    q_maps = []
    for h in range(heads):
        qt = q_ref[0, h * V_HEAD:(h + 1) * V_HEAD, :]
        zero = jnp.zeros_like(qt)
        q_maps += [jnp.where(row < HEAD_DIM, qt, zero), jnp.where(row >= HEAD_DIM, qt, zero)]
    acc_scr[...] = jnp.zeros_like(acc_scr)
    m_scr[...] = jnp.full_like(m_scr, MASK_VALUE)
    l_scr[...] = jnp.zeros_like(l_scr)

    kv_chunk = lax.broadcasted_iota(jnp.int32, (blk, blk), 0) >> CAUSAL_CHUNK_SHIFT
    q_chunk = lax.broadcasted_iota(jnp.int32, (blk, blk), 1) >> CAUSAL_CHUNK_SHIFT

    def scores(j, allowed):
        rows = pl.ds(pl.multiple_of(j * blk, blk), blk)
        for n in range(n_streams):
            h = n // 2
            kt = k_ref[0, rows, h * V_HEAD:(h + 1) * V_HEAD]
            s = jnp.dot(kt, q_maps[n], preferred_element_type=F32)
            if allowed is not None:
                s = jnp.where(allowed, s, MASK_VALUE)
            s_scr[n] = s
            c_scr[n] = jnp.max(s, axis=0, keepdims=True)

    def probs():
        out = []
        for n in range(n_streams):
            m_old = m_scr[n]
            m_new = jnp.maximum(m_old, c_scr[n])
            alpha = jnp.exp2(m_old - m_new)
            p = jnp.exp2(s_scr[n] - m_new)
            l_scr[n] = alpha * l_scr[n] + jnp.sum(p, axis=0, keepdims=True)
            m_scr[n] = m_new
            out.append((alpha, p.astype(BF16)))
        return out

    def accumulate(j, alpha_p):
        cols = pl.ds(pl.multiple_of(j * blk, blk), blk)
        for n, (alpha, p) in enumerate(alpha_p):
            h = n // 2
            vt = v_ref[0, h * V_HEAD:(h + 1) * V_HEAD, cols]
            acc_scr[n] = alpha * acc_scr[n] + jnp.dot(vt, p, preferred_element_type=F32)

    def step(j, next_allowed):
        alpha_p = probs()
        scores(j + 1, next_allowed)
        accumulate(j, alpha_p)

    scores(0, kv_chunk <= q_chunk + qi * (blk >> CAUSAL_CHUNK_SHIFT))

    def body(j, carry):
        step(j, None)
        return carry

    lax.fori_loop(0, qi - 1, body, 0)

    @pl.when(qi >= 1)
    def _():
        step(qi - 1, kv_chunk <= q_chunk)

    accumulate(qi, probs())

    lam_v = lam_ref[...]
    e1 = jnp.exp(jnp.sum(lam_v[0:1] * lam_v[1:2], axis=1, keepdims=True))
    e2 = jnp.exp(jnp.sum(lam_v[2:3] * lam_v[3:4], axis=1, keepdims=True))
    lam = e1 - e2 + lam_init
    for h in range(heads):
        o = (acc_scr[2 * h] / l_scr[2 * h]
             - lam * (acc_scr[2 * h + 1] / l_scr[2 * h + 1]))
        ms = jnp.mean(o * o, axis=0, keepdims=True)
        o = o * lax.rsqrt(ms + NORM_EPS) * hg_ref[...] * (1.0 - lam_init)
        head_rows = slice(h * V_HEAD, (h + 1) * V_HEAD)
        o_ref[0, head_rows, :] = (o * ga_ref[0, head_rows, :].astype(F32)).astype(BF16)


def _diff_attention(lam_vecs, head_g, q_t, k, v_t, ga_t, *, blk, heads, lam_init):
    b, _, s = q_t.shape
    nq = s // blk
    n_streams = 2 * heads
    width = heads * V_HEAD
    head_tile = pl.BlockSpec((1, width, blk), lambda bi, h, i: (bi, h, i))
    return pl.pallas_call(
        functools.partial(_attn_kernel, blk=blk, heads=heads, lam_init=lam_init),
        grid=(b, N_HEADS // heads, nq),
        in_specs=[pl.BlockSpec((4, HEAD_DIM), lambda bi, h, i: (0, 0)),
                  pl.BlockSpec((V_HEAD, 1), lambda bi, h, i: (0, 0)),
                  head_tile,
                  pl.BlockSpec((1, s, width), lambda bi, h, i: (bi, 0, h)),
                  pl.BlockSpec((1, width, s), lambda bi, h, i: (bi, h, 0)),
                  head_tile],
        out_specs=head_tile,
        out_shape=jax.ShapeDtypeStruct((b, D_MODEL, s), BF16),
        scratch_shapes=[pltpu.VMEM((n_streams, V_HEAD, blk), F32),
                        pltpu.VMEM((n_streams, 1, blk), F32),
                        pltpu.VMEM((n_streams, 1, blk), F32),
                        pltpu.VMEM((n_streams, blk, blk), F32),
                        pltpu.VMEM((n_streams, 1, blk), F32)],
        compiler_params=pltpu.CompilerParams(
            dimension_semantics=("parallel", "parallel", "arbitrary"), vmem_limit_bytes=VMEM_LIMIT),
        name="diff_attn",
    )(lam_vecs, head_g, q_t, k, v_t, ga_t)


def _zoh(a_re, a_im, log_dt):
    dt = jnp.exp(log_dt)
    mag = jnp.exp(a_re * dt)
    lre = mag * jnp.cos(a_im * dt)
    lim = mag * jnp.sin(a_im * dt)
    nr = lre - 1.0
    den = a_re * a_re + a_im * a_im
    fre = (nr * a_re + lim * a_im) / den
    fim = (lim * a_re - nr * a_im) / den
    return lre, lim, fre, fim


def _cpow(bre, bim, e, nbits):
    bre = jnp.broadcast_to(bre, e.shape)
    bim = jnp.broadcast_to(bim, e.shape)
    rre = jnp.ones(e.shape, F32)
    rim = jnp.zeros(e.shape, F32)
    for bit in range(nbits):
        take = ((e >> bit) & 1) == 1
        rre, rim = (jnp.where(take, rre * bre - rim * bim, rre),
                    jnp.where(take, rre * bim + rim * bre, rim))
        bre, bim = bre * bre - bim * bim, 2.0 * bre * bim
    return rre, rim


def _ssm_kernel(u_ref, prow_ref, pcol_ref, bre_ref, bim_ref, x1_ref, x2_ref, cre_ref, cim_ref,
                uu_ref, vv_ref, d_ref, y_ref, kt_ref, pq_ref, t_ref, b_ref, yacc_ref, st_ref, *, n_chunks, groups):
    nbc = u_ref.shape[2]
    n_log = SSM_CHUNK.bit_length() - 1
    n_pairs = SSM_GROUP // 2
    lane = lax.broadcasted_iota(jnp.int32, (LANES, LANES), 1)
    sub = lax.broadcasted_iota(jnp.int32, (LANES, LANES), 0)
    causal = lane >= sub

    def setup(g):
        prow = prow_ref[g]
        lre, lim, fre, fim = _zoh(prow[0:1], prow[1:2], prow[2:3])
        pcol = pcol_ref[g]
        lre_c, lim_c, _, _ = _zoh(pcol[:, 0:1], pcol[:, 1:2], pcol[:, 2:3])
        ltre, ltim = _cpow(lre_c, lim_c, lane, n_log)
        lt1re = ltre * lre_c - ltim * lim_c
        lt1im = ltre * lim_c + ltim * lre_c
        lsre, lsim = _cpow(lre, lim, (SSM_CHUNK - 1) - sub, n_log)

        fre_h = fre[:, :SSM_STATE]
        fim_h = fim[:, :SSM_STATE]
        bbre = fre_h * bre_ref[g] - fim_h * bim_ref[g]
        bbim = fre_h * bim_ref[g] + fim_h * bre_ref[g]
        pair = (SSM_GROUP, SSM_GROUP, SSM_STATE)
        flat = (SSM_GROUP * SSM_GROUP, SSM_STATE)
        c_re = jnp.broadcast_to(cre_ref[g][:, None, :], pair).reshape(flat)
        c_im = jnp.broadcast_to(cim_ref[g][:, None, :], pair).reshape(flat)
        b_re = jnp.broadcast_to(bbre[None], pair).reshape(flat)
        b_im = jnp.broadcast_to(bbim[None], pair).reshape(flat)
        mre = c_re * b_re - c_im * b_im
        mim = c_re * b_im + c_im * b_re
        kt_ref[g] = (jnp.dot(mre, ltre[:SSM_STATE], preferred_element_type=F32, precision=lax.Precision.HIGHEST)
                     - jnp.dot(mim, ltim[:SSM_STATE], preferred_element_type=F32, precision=lax.Precision.HIGHEST))
        pq_ref[g, 0] = fre * x1_ref[g] + fim * x2_ref[g]
        pq_ref[g, 1] = fre * x2_ref[g] - fim * x1_ref[g]
        return lre, lim, lsre, lsim, lt1re, lt1im

    tables = [setup(g) for g in range(groups)]

    def build(g, i):
        _, _, lsre, lsim, _, _ = tables[g]
        t_rows, b_rows = [], []
        for dh in range(2):
            hi = 2 * i + dh
            blocks = []
            for ho in range(SSM_GROUP):
                lag = jnp.broadcast_to(kt_ref[g, pl.ds(ho * SSM_GROUP + hi, 1), :], (LANES, LANES))
                toep = pltpu.roll(lag, 0, 1, stride=1, stride_axis=0)
                blocks.append(jnp.where(causal, toep, 0.0).astype(BF16))
            t_rows.append(jnp.concatenate(blocks, axis=1))
            b_rows.append((lsre * pq_ref[g, 0, pl.ds(hi, 1), :] + lsim * pq_ref[g, 1, pl.ds(hi, 1), :]).astype(BF16))
        t_ref[g] = jnp.concatenate(t_rows, axis=0)
        b_ref[g] = jnp.concatenate(b_rows, axis=0)

    for g in range(groups):
        build(g, 0)
    yacc_ref[...] = jnp.zeros_like(yacc_ref)
    st_ref[...] = jnp.zeros_like(st_ref)

    def pair_body(i, carry):
        for g in range(groups):
            x = jnp.concatenate([u_ref[g, 2 * i], u_ref[g, 2 * i + 1]], axis=1)
            yacc_ref[g] += jnp.dot(x, t_ref[g], preferred_element_type=F32)
            st_ref[g] += jnp.dot(x, b_ref[g], preferred_element_type=F32)
        for g in range(groups):
            build(g, jnp.minimum(i + 1, n_pairs - 1))
        return carry

    lax.fori_loop(0, n_pairs, pair_body, 0)

    half_sign = jnp.where(lax.broadcasted_iota(jnp.int32, (1, LANES), 1) < SSM_STATE, -1.0, 1.0)
    chunk_id = lax.broadcasted_iota(jnp.int32, (nbc, LANES), 0) % n_chunks
    for g in range(groups):
        lre, lim, _, _, lt1re, lt1im = tables[g]
        are, aim = lre, lim
        for _ in range(n_log):
            are, aim = are * are - aim * aim, 2.0 * are * aim
        acc = st_ref[g]
        dist = 1
        while dist < n_chunks:
            prev = jnp.where(chunk_id >= dist, pltpu.roll(acc, dist, 0), 0.0)
            acc = acc + are * prev + (aim * half_sign) * pltpu.roll(prev, SSM_STATE, 1)
            are, aim = are * are - aim * aim, 2.0 * are * aim
            dist *= 2
        h_in = jnp.where(chunk_id >= 1, pltpu.roll(acc, 1, 0), 0.0).astype(BF16)

        cm = [(lt1re * uu_ref[g][:, ho:ho + 1] + lt1im * vv_ref[g][:, ho:ho + 1]).astype(BF16)
              for ho in range(SSM_GROUP)]
        y_all = yacc_ref[g] + jnp.dot(h_in, jnp.concatenate(cm, axis=1), preferred_element_type=F32)

        for o in range(SSM_GROUP // SUBLANES):
            ys = [y_all[:, h * LANES:(h + 1) * LANES] + d_ref[g, h] * u_ref[g, h].astype(F32)
                  for h in range(o * SUBLANES, (o + 1) * SUBLANES)]
            y_ref[g, o] = jnp.swapaxes(jnp.stack(ys, axis=0), 0, 1)


def _s5_groups(u_t, a_re, a_im, log_dt, b_re, b_im, c_re, c_im, d_skip, *, n_chunks):
    nbc = u_t.shape[1] // SSM_CHUNK
    u4 = u_t.reshape(SSM_GROUPS, SSM_GROUP, nbc, SSM_CHUNK)
    g = SSM_GROUPS
    dup = lambda v: jnp.concatenate([v, v], axis=-1)
    ldt = jnp.broadcast_to(log_dt[:, None], (g, LANES))
    prow = jnp.concatenate([jnp.stack([dup(a_re), dup(a_im), ldt], axis=1),
                            jnp.zeros((g, 5, LANES), F32)], axis=1)
    pcol = prow.transpose(0, 2, 1)
    bre_t = b_re.transpose(0, 2, 1)
    bim_t = b_im.transpose(0, 2, 1)
    x1 = jnp.concatenate([bre_t, bim_t], axis=-1)
    x2 = jnp.concatenate([-bim_t, bre_t], axis=-1)
    cre_t = c_re.transpose(0, 2, 1)
    cim_t = c_im.transpose(0, 2, 1)
    uu = jnp.concatenate([cre_t, -cim_t], axis=1)
    vv = jnp.concatenate([-cim_t, -cre_t], axis=1)
    d_b = jnp.broadcast_to(d_skip.reshape(g, SSM_GROUP, 1, 1), (g, SSM_GROUP, 1, LANES))

    gps = SSM_GROUPS_PER_STEP

    def per_group(*tail):
        return pl.BlockSpec((gps,) + tail, lambda gi: (gi,) + (0,) * len(tail))

    y4 = pl.pallas_call(
        functools.partial(_ssm_kernel, n_chunks=n_chunks, groups=gps),
        grid=(g // gps,),
        in_specs=[per_group(SSM_GROUP, nbc, SSM_CHUNK), per_group(8, LANES), per_group(LANES, 8),
                  per_group(SSM_GROUP, SSM_STATE), per_group(SSM_GROUP, SSM_STATE),
                  per_group(SSM_GROUP, LANES), per_group(SSM_GROUP, LANES),
                  per_group(SSM_GROUP, SSM_STATE), per_group(SSM_GROUP, SSM_STATE),
                  per_group(LANES, SSM_GROUP), per_group(LANES, SSM_GROUP),
                  per_group(SSM_GROUP, 1, LANES)],
        out_specs=per_group(SSM_GROUP // SUBLANES, nbc, SUBLANES, SSM_CHUNK),
        out_shape=jax.ShapeDtypeStruct((g, SSM_GROUP // SUBLANES, nbc, SUBLANES, SSM_CHUNK), F32),
        scratch_shapes=[pltpu.VMEM((gps, SSM_GROUP * SSM_GROUP, LANES), F32),
                        pltpu.VMEM((gps, 2, SSM_GROUP, LANES), F32),
                        pltpu.VMEM((gps, 2 * SSM_CHUNK, SSM_GROUP * SSM_CHUNK), BF16),
                        pltpu.VMEM((gps, 2 * SSM_CHUNK, LANES), BF16),
                        pltpu.VMEM((gps, nbc, SSM_GROUP * SSM_CHUNK), F32),
                        pltpu.VMEM((gps, nbc, LANES), F32)],
        compiler_params=pltpu.CompilerParams(
            dimension_semantics=("parallel",), vmem_limit_bytes=VMEM_LIMIT),
        name="s5_groups",
    )(u4, prow, pcol, bre_t, bim_t, x1, x2, c_re, c_im, uu, vv, d_b)
    return y4


def _out_kernel(o_ref, y_ref, gs_ref, x_ref, gate_ref, wg_ref, bg_ref, wo_ref, out_ref):
    mixed_attn = jnp.dot(wo_ref[:, :D_MODEL], o_ref[0], preferred_element_type=F32)
    n_chunks = y_ref.shape[2]
    y = jnp.concatenate([y_ref[:, :, c].reshape(D_MODEL, SSM_CHUNK) for c in range(n_chunks)], axis=1)
    z0 = jax.nn.gelu(y)
    t = jnp.dot(wg_ref[...], z0.astype(BF16), preferred_element_type=F32) + bg_ref[...]
    z = z0 * jax.nn.sigmoid(t) * gs_ref[...].astype(F32)
    mixed_t = mixed_attn + jnp.dot(wo_ref[:, D_MODEL:], z.astype(BF16), preferred_element_type=F32)
    out_ref[0] = x_ref[0] + gate_ref[0] * mixed_t.T


def _mix_out(o_t, y4, gs_t, x, gate, w_glu_t, b_glu, w_out_t, *, tn):
    b, s, _ = x.shape
    nt = s // tn
    flat_ch_major = pl.BlockSpec((D_MODEL, tn), lambda bi, i: (0, bi * nt + i))
    tok_major = pl.BlockSpec((1, tn, D_MODEL), lambda bi, i: (bi, i, 0))
    chunk_major = pl.BlockSpec((SSM_GROUPS, SSM_GROUP // SUBLANES, tn // SSM_CHUNK, SUBLANES, SSM_CHUNK),
                               lambda bi, i: (0, 0, bi * nt + i, 0, 0))
    return pl.pallas_call(
        _out_kernel,
        grid=(b, nt),
        in_specs=[pl.BlockSpec((1, D_MODEL, tn), lambda bi, i: (bi, 0, i)),
                  chunk_major, flat_ch_major, tok_major,
                  pl.BlockSpec((1, 1, D_MODEL), lambda bi, i: (bi, 0, 0)),
                  pl.BlockSpec((D_MODEL, D_MODEL), lambda bi, i: (0, 0), pipeline_mode=pl.Buffered(1)),
                  pl.BlockSpec((D_MODEL, 1), lambda bi, i: (0, 0)),
                  pl.BlockSpec((D_MODEL, 2 * D_MODEL), lambda bi, i: (0, 0), pipeline_mode=pl.Buffered(1))],
        out_specs=tok_major,
        out_shape=jax.ShapeDtypeStruct(x.shape, x.dtype),
        compiler_params=pltpu.CompilerParams(
            dimension_semantics=("parallel", "parallel"), vmem_limit_bytes=VMEM_LIMIT),
        name="mix_out",
    )(o_t, y4, gs_t, x, gate, w_glu_t, b_glu, w_out_t)


def kernel(x, c, w_ada, b_ada, norm_g, w_in, q_norm_g, k_norm_g, lam_q1, lam_k1, lam_q2, lam_k2,
           head_norm_g, ssm_a_re, ssm_a_im, ssm_log_dt, ssm_b_re, ssm_b_im, ssm_c_re, ssm_c_im,
           ssm_d, w_glu, b_glu, w_out):
    b, s, _ = x.shape
    depth = w_in.shape[0]
    tok_tile = min(512, s)
    attn_tile = min(512, s)
    assert s % tok_tile == 0 and s % attn_tile == 0 and s % SSM_CHUNK == 0 and tok_tile % LANES == 0

    pos = jnp.arange(s, dtype=F32)
    inv_freq = 1.0 / (ROPE_THETA ** (jnp.arange(0, HEAD_DIM, 2, dtype=F32) / HEAD_DIM))
    ang_t = inv_freq[:, None] * pos[None, :]
    cos_t, sin_t = jnp.cos(ang_t), jnp.sin(ang_t)

    for l in range(depth):
        lam_init = 0.8 - 0.6 * math.exp(-0.3 * l)
        mod = _ada_mod(c, w_ada[l], b_ada[l])
        shift, scale, gate = (mod[:, i * D_MODEL:(i + 1) * D_MODEL].reshape(b, 1, D_MODEL) for i in range(3))
        q_t, k, v_t, ga_t, u_t, gs_t = _in_proj(
            x, shift, scale, norm_g[l].reshape(1, D_MODEL), w_in[l].T.astype(BF16), cos_t, sin_t,
            q_norm_g[l].reshape(HEAD_DIM, 1), k_norm_g[l].reshape(HEAD_DIM, 1), tm=min(2 * tok_tile, s))
        lam_vecs = jnp.stack([lam_q1[l], lam_k1[l], lam_q2[l], lam_k2[l]]).astype(F32)
        o_t = _diff_attention(lam_vecs, head_norm_g[l].reshape(V_HEAD, 1), q_t, k, v_t, ga_t,
                              blk=attn_tile, heads=ATTN_HEADS_PER_STEP, lam_init=lam_init)
        y4 = _s5_groups(u_t, ssm_a_re[l], ssm_a_im[l], ssm_log_dt[l], ssm_b_re[l], ssm_b_im[l],
                         ssm_c_re[l], ssm_c_im[l], ssm_d[l], n_chunks=s // SSM_CHUNK)
        x = _mix_out(o_t, y4, gs_t, x, gate, w_glu[l].T.astype(BF16), b_glu[l].reshape(D_MODEL, 1),
                     w_out[l].T.astype(BF16), tn=min(2 * tok_tile, s))
    return x
```
